```python
import jax, jax.numpy as jnp
from jax import lax
import numpy as np

D_MODEL = 1024
BATCH = 4
SEQ = 4096
DEPTH = 4

CONV_WIDTH = 31
RET_HEADS = 4
RET_QK_DIM = D_MODEL
RET_V_DIM = 2 * D_MODEL
RET_HEAD_QK = RET_QK_DIM // RET_HEADS
RET_HEAD_V = RET_V_DIM // RET_HEADS
RET_CHUNK = 128
ROPE_BASE = 10000.0
FFN_DIM = 2816
FFN_CONV_WIDTH = 3
N_MIXERS = 2
N_CONV_LAYERS = (DEPTH + 1) // 2
N_RET_LAYERS = DEPTH // 2
EPS = 1e-6

kernel_name = "hybrid_conformer_conv_retnet_convffn"


def rmsnorm(x, g):
    xf = x.astype(jnp.float32)
    y = xf * lax.rsqrt(jnp.mean(xf * xf, axis=-1, keepdims=True) + EPS)
    return y.astype(x.dtype) * g


def causal_dwconv(x, w, b):
    k = w.shape[0]
    y = lax.conv_general_dilated(
        x, w[:, None, :].astype(x.dtype), window_strides=(1,), padding=[(k - 1, 0)],
        dimension_numbers=("NWC", "WIO", "NWC"), feature_group_count=x.shape[-1])
    return y + b


def conv_module(x, w_in, dw_w, dw_b, ln_g, ln_b, w_out):
    h = x @ w_in
    a, gate = jnp.split(h, 2, axis=-1)
    h = a * jax.nn.sigmoid(gate)
    h = causal_dwconv(h, dw_w, dw_b)
    hf = h.astype(jnp.float32)
    mu = jnp.mean(hf, axis=-1, keepdims=True)
    var = jnp.mean(jnp.square(hf - mu), axis=-1, keepdims=True)
    h = ((hf - mu) * lax.rsqrt(var + EPS)).astype(x.dtype) * ln_g + ln_b
    h = jax.nn.silu(h)
    return h @ w_out


def rotary(t, positions):
    d = t.shape[-1]
    inv_freq = 1.0 / (ROPE_BASE ** (jnp.arange(0, d, 2, dtype=jnp.float32) / d))
    ang = positions.astype(jnp.float32)[..., None] * inv_freq
    cos = jnp.cos(ang)[:, :, None, :]
    sin = jnp.sin(ang)[:, :, None, :]
    tf = t.astype(jnp.float32)
    t1, t2 = tf[..., : d // 2], tf[..., d // 2:]
    return jnp.concatenate([t1 * cos - t2 * sin, t2 * cos + t1 * sin], axis=-1).astype(t.dtype)


def retention(x, positions, w_in, gn_g, w_out):
    b, s, _ = x.shape
    nc = s // RET_CHUNK
    h = x @ w_in
    q, k, v, g = jnp.split(h, [RET_QK_DIM, 2 * RET_QK_DIM, 2 * RET_QK_DIM + RET_V_DIM], axis=-1)
    q = rotary(q.reshape(b, s, RET_HEADS, RET_HEAD_QK), positions)
    k = rotary(k.reshape(b, s, RET_HEADS, RET_HEAD_QK), positions) * (RET_HEAD_QK ** -0.5)
    v = v.reshape(b, s, RET_HEADS, RET_HEAD_V)

    def to_chunks(t):
        return t.reshape(b, nc, RET_CHUNK, RET_HEADS, t.shape[-1]).transpose(1, 0, 3, 2, 4)

    qc, kc, vc = to_chunks(q), to_chunks(k), to_chunks(v)
    dt = q.dtype
    log_gamma = jnp.log1p(-jnp.exp2(-5.0 - jnp.arange(RET_HEADS, dtype=jnp.float32)))
    idx = jnp.arange(RET_CHUNK, dtype=jnp.float32)
    dist = idx[:, None] - idx[None, :]
    intra = jnp.where(dist[None] >= 0,
                      jnp.exp(log_gamma[:, None, None] * jnp.maximum(dist, 0.0)[None]), 0.0).astype(dt)
    q_decay = jnp.exp(log_gamma[:, None] * (idx + 1.0)[None]).astype(dt)[:, :, None]
    k_decay = jnp.exp(log_gamma[:, None] * (RET_CHUNK - 1.0 - idx)[None]).astype(dt)[:, :, None]
    chunk_decay = jnp.exp(log_gamma * RET_CHUNK).astype(dt)[:, None, None]

    def step(state, qkv):
        qi, ki, vi = qkv
        scores = jnp.einsum("bhnd,bhmd->bhnm", qi, ki) * intra
        inner = jnp.einsum("bhnm,bhme->bhne", scores, vi)
        cross = jnp.einsum("bhnd,bhde->bhne", qi, state) * q_decay
        new_state = chunk_decay * state + jnp.einsum("bhmd,bhme->bhde", ki * k_decay, vi)
        return new_state, inner + cross

    state0 = jnp.zeros((b, RET_HEADS, RET_HEAD_QK, RET_HEAD_V), dtype=dt)
    _, o = lax.scan(step, state0, (qc, kc, vc))
    o = o.transpose(1, 0, 3, 2, 4).reshape(b, s, RET_HEADS, RET_HEAD_V)
    of = o.astype(jnp.float32)
    mu = jnp.mean(of, axis=-1, keepdims=True)
    var = jnp.mean(jnp.square(of - mu), axis=-1, keepdims=True)
    o = ((of - mu) * lax.rsqrt(var + EPS)).astype(x.dtype).reshape(b, s, RET_V_DIM) * gn_g
    return (jax.nn.silu(g) * o) @ w_out


def conv_ffn(x, w_in, dw_w, dw_b, w_out):
    h = x @ w_in
    a, u = jnp.split(h, 2, axis=-1)
    a = causal_dwconv(a, dw_w, dw_b)
    return (jax.nn.silu(a) * u) @ w_out


def setup_inputs(seed: int = 0) -> dict:
    key = jax.random.key(seed)
    ks = jax.random.split(key, 20)
    D, F = D_MODEL, FFN_DIM
    na, nb = N_CONV_LAYERS, N_RET_LAYERS

    def nrm(k, shape, scale):
        return jax.random.normal(k, shape, dtype=jnp.float32) * scale

    ret_in = 2 * RET_QK_DIM + 2 * RET_V_DIM
    return {
        "x": nrm(ks[0], (BATCH, SEQ, D), 1.0),
        "positions": jnp.broadcast_to(jnp.arange(SEQ, dtype=jnp.int32)[None, :], (BATCH, SEQ)),
        "conv_w_in": nrm(ks[1], (na, D, 2 * D), D ** -0.5),
        "conv_dw_w": nrm(ks[2], (na, CONV_WIDTH, D), CONV_WIDTH ** -0.5),
        "conv_dw_b": nrm(ks[3], (na, D), 0.01),
        "conv_ln_g": 1.0 + nrm(ks[4], (na, D), 0.01),
        "conv_ln_b": nrm(ks[5], (na, D), 0.01),
        "conv_w_out": nrm(ks[6], (na, D, D), D ** -0.5),
        "ret_w_in": nrm(ks[7], (nb, D, ret_in), D ** -0.5),
        "ret_gn_g": 1.0 + nrm(ks[8], (nb, RET_V_DIM), 0.01),
        "ret_w_out": nrm(ks[9], (nb, RET_V_DIM, D), RET_V_DIM ** -0.5),
        "ffn_w_in": nrm(ks[10], (DEPTH, D, 2 * F), D ** -0.5),
        "ffn_dw_w": nrm(ks[11], (DEPTH, FFN_CONV_WIDTH, F), FFN_CONV_WIDTH ** -0.5),
        "ffn_dw_b": nrm(ks[12], (DEPTH, F), 0.01),
        "ffn_w_out": nrm(ks[13], (DEPTH, F, D), F ** -0.5),
        "norm_mix_g": 1.0 + nrm(ks[14], (DEPTH, D), 0.01),
        "norm_ffn_g": 1.0 + nrm(ks[15], (DEPTH, D), 0.01),
        "final_g": 1.0 + nrm(ks[16], (D,), 0.01),
    }


def reference(x, positions, conv_w_in, conv_dw_w, conv_dw_b, conv_ln_g, conv_ln_b, conv_w_out,
              ret_w_in, ret_gn_g, ret_w_out, ffn_w_in, ffn_dw_w, ffn_dw_b, ffn_w_out,
              norm_mix_g, norm_ffn_g, final_g):
    for i in range(DEPTH):
        h = rmsnorm(x, norm_mix_g[i])
        j = i // N_MIXERS
        if i % N_MIXERS == 0:
            x = x + conv_module(h, conv_w_in[j], conv_dw_w[j], conv_dw_b[j],
                                conv_ln_g[j], conv_ln_b[j], conv_w_out[j])
        else:
            x = x + retention(h, positions, ret_w_in[j], ret_gn_g[j], ret_w_out[j])
        h = rmsnorm(x, norm_ffn_g[i])
        x = x + conv_ffn(h, ffn_w_in[i], ffn_dw_w[i], ffn_dw_b[i], ffn_w_out[i])
    return rmsnorm(x, final_g)
```

```python
import functools

import jax
import jax.numpy as jnp
from jax import lax
from jax.experimental import pallas as pl
from jax.experimental.pallas import tpu as pltpu

D_MODEL = 1024
CONV_WIDTH = 31
RET_HEADS = 4
RET_HEAD_QK = 256
RET_HEAD_V = 512
RET_QK_DIM = RET_HEADS * RET_HEAD_QK
RET_V_DIM = RET_HEADS * RET_HEAD_V
RET_CHUNK = 128
ROPE_BASE = 10000.0
FFN_DIM = 2816
FFN_CONV_WIDTH = 3
EPS = 1e-6

F32 = jnp.float32
BF16 = jnp.bfloat16

V7X_VMEM_BYTES = 64 * 1024 * 1024
VMEM_LIMIT_BYTES = V7X_VMEM_BYTES - 8 * 1024 * 1024
SUBLANES = 8
LANES = 128

TOKEN_TILE = 512
FFN_COL_TILE = 256
CONV_HIST = 32
CONV_ROW_BLOCK = 64


def _resident(shape):
    zeros = (0,) * len(shape)
    return pl.BlockSpec(shape, lambda *_: zeros, pipeline_mode=pl.Buffered(1))


def _token_spec(tm, width):
    return pl.BlockSpec((1, tm, width), lambda b, i: (b, i, 0))


def _compiler_params():
    return pltpu.CompilerParams(
        dimension_semantics=("arbitrary", "arbitrary"),
        vmem_limit_bytes=VMEM_LIMIT_BYTES,
    )


def _rmsnorm(x, g):
    ms = jnp.mean(x * x, axis=-1, keepdims=True)
    return x * lax.rsqrt(ms + EPS) * g


def _center_scale(x):
    mu = jnp.mean(x, axis=-1, keepdims=True)
    xc = x - mu
    var = jnp.mean(xc * xc, axis=-1, keepdims=True)
    return xc * lax.rsqrt(var + EPS)


def _sigmoid(x):
    return 1.0 / (1.0 + jnp.exp(-x))


def _shift_rows(a, prev, k, row):
    out = pltpu.roll(a, k, 0)
    for r in range(k):
        out = jnp.where(row == r, prev[SUBLANES - k + r:SUBLANES - k + r + 1, :], out)
    return out


def _ffn_kernel(x_ref, g_ref, win_ref, dww_ref, dwb_ref, wout_ref, o_ref,
                hb_ref, act_ref, carry_ref, *, tm, fc):
    x = x_ref[0]
    hb_ref[...] = _rmsnorm(x, g_ref[...]).astype(BF16)

    @pl.when(pl.program_id(1) == 0)
    def _():
        carry_ref[...] = jnp.zeros_like(carry_ref)

    row = lax.broadcasted_iota(jnp.int32, (tm, fc), 0)
    for c in range(FFN_DIM // fc):
        lo = c * fc
        hb = hb_ref[...]
        a = jnp.dot(hb, win_ref[:, lo:lo + fc], preferred_element_type=F32)
        u = jnp.dot(hb, win_ref[:, FFN_DIM + lo:FFN_DIM + lo + fc],
                    preferred_element_type=F32)
        prev = carry_ref[:, lo:lo + fc]
        carry_ref[:, lo:lo + fc] = a[tm - SUBLANES:, :]
        a1 = _shift_rows(a, prev, 1, row)
        a2 = _shift_rows(a, prev, 2, row)
        conv = (dww_ref[2:3, lo:lo + fc] * a + dww_ref[1:2, lo:lo + fc] * a1
                + dww_ref[0:1, lo:lo + fc] * a2 + dwb_ref[:, lo:lo + fc])
        act_ref[:, lo:lo + fc] = (conv * _sigmoid(conv) * u).astype(BF16)

    out = jnp.dot(act_ref[...], wout_ref[...], preferred_element_type=F32)
    o_ref[0] = x + out


def _conv_ffn(x, g, w_in, dw_w, dw_b, w_out, *, tm=TOKEN_TILE, fc=FFN_COL_TILE):
    b, s, d = x.shape
    f = FFN_DIM
    return pl.pallas_call(
        functools.partial(_ffn_kernel, tm=tm, fc=fc),
        grid=(b, s // tm),
        in_specs=[
            _token_spec(tm, d),
            _resident((1, d)),
            _resident((d, 2 * f)),
            _resident((FFN_CONV_WIDTH, f)),
            _resident((1, f)),
            _resident((f, d)),
        ],
        out_specs=_token_spec(tm, d),
        out_shape=jax.ShapeDtypeStruct(x.shape, x.dtype),
        scratch_shapes=[
            pltpu.VMEM((tm, d), BF16),
            pltpu.VMEM((tm, f), BF16),
            pltpu.VMEM((SUBLANES, f), F32),
        ],
        compiler_params=_compiler_params(),
        name="conv_ffn",
    )(x, g.reshape(1, d), w_in.astype(BF16), dw_w, dw_b.reshape(1, f), w_out.astype(BF16))


def _convmod_kernel(x_ref, g_ref, win_ref, dww_ref, dwb_ref, lng_ref, lnb_ref, wout_ref,
                    o_ref, buf_ref, y_ref, *, tm):
    d = D_MODEL
    x = x_ref[0]
    hb = _rmsnorm(x, g_ref[...]).astype(BF16)

    @pl.when(pl.program_id(1) == 0)
    def _():
        buf_ref[0:CONV_HIST, :] = jnp.zeros((CONV_HIST, d), F32)

    @pl.when(pl.program_id(1) > 0)
    def _():
        buf_ref[0:CONV_HIST, :] = buf_ref[tm:tm + CONV_HIST, :]

    a = jnp.dot(hb, win_ref[:, :d], preferred_element_type=F32)
    gate = jnp.dot(hb, win_ref[:, d:], preferred_element_type=F32)
    buf_ref[CONV_HIST:CONV_HIST + tm, :] = a * _sigmoid(gate)

    base = CONV_HIST - (CONV_WIDTH - 1)
    rb = CONV_ROW_BLOCK
    for r0 in range(0, tm, rb):
        for c0 in range(0, d, LANES):
            total = None
            for phase in range(SUBLANES):
                part = None
                for j in range(CONV_WIDTH):
                    off = base + j
                    if off % SUBLANES != phase:
                        continue
                    term = (dww_ref[j:j + 1, c0:c0 + LANES]
                            * buf_ref[r0 + off:r0 + off + rb, c0:c0 + LANES])
                    part = term if part is None else part + term
                if part is not None:
                    total = part if total is None else total + part
            y_ref[r0:r0 + rb, c0:c0 + LANES] = total + dwb_ref[:, c0:c0 + LANES]

    yn = _center_scale(y_ref[...]) * lng_ref[...] + lnb_ref[...]
    act = (yn * _sigmoid(yn)).astype(BF16)
    o_ref[0] = x + jnp.dot(act, wout_ref[...], preferred_element_type=F32)


def _conv_module(x, g, w_in, dw_w, dw_b, ln_g, ln_b, w_out, *, tm=TOKEN_TILE):
    b, s, d = x.shape
    return pl.pallas_call(
        functools.partial(_convmod_kernel, tm=tm),
        grid=(b, s // tm),
        in_specs=[
            _token_spec(tm, d),
            _resident((1, d)),
            _resident((d, 2 * d)),
            _resident((CONV_WIDTH, d)),
            _resident((1, d)),
            _resident((1, d)),
            _resident((1, d)),
            _resident((d, d)),
        ],
        out_specs=_token_spec(tm, d),
        out_shape=jax.ShapeDtypeStruct(x.shape, x.dtype),
        scratch_shapes=[
            pltpu.VMEM((CONV_HIST + tm, d), F32),
            pltpu.VMEM((tm, d), F32),
        ],
        compiler_params=_compiler_params(),
        name="conv_module",
    )(x, g.reshape(1, d), w_in.astype(BF16), dw_w, dw_b.reshape(1, d),
      ln_g.reshape(1, d), ln_b.reshape(1, d), w_out.astype(BF16))


def _rope_kernel(pos_ref, invf_ref, cos_ref, sin_ref):
    ang = pos_ref[0] * invf_ref[...]
    cos_ref[0] = jnp.cos(ang)
    sin_ref[0] = jnp.sin(ang)


def _rope_tables(positions, *, tm=TOKEN_TILE):
    b, s = positions.shape
    half = RET_HEAD_QK // 2
    inv_freq = 1.0 / (ROPE_BASE ** (jnp.arange(0, RET_HEAD_QK, 2, dtype=F32) / RET_HEAD_QK))
    table = jax.ShapeDtypeStruct((b, s, half), F32)
    return pl.pallas_call(
        _rope_kernel,
        grid=(b, s // tm),
        in_specs=[_token_spec(tm, 1), _resident((1, half))],
        out_specs=[_token_spec(tm, half), _token_spec(tm, half)],
        out_shape=[table, table],
        compiler_params=_compiler_params(),
        name="rope_tables",
    )(positions.astype(F32).reshape(b, s, 1), inv_freq.reshape(1, half))


def _ret_proj_kernel(x_ref, g_ref, win_ref, cos_ref, sin_ref,
                     q_ref, k_ref, v_ref, sg_ref, hb_ref):
    hb_ref[...] = _rmsnorm(x_ref[0], g_ref[...]).astype(BF16)
    cos = cos_ref[0]
    sin = sin_ref[0]
    half = RET_HEAD_QK // 2

    def rotary(col, scale, out_ref, out_col):
        t = jnp.dot(hb_ref[...], win_ref[:, col:col + RET_HEAD_QK], preferred_element_type=F32)
        t1, t2 = t[:, :half], t[:, half:]
        out_ref[0, :, out_col:out_col + half] = ((t1 * cos - t2 * sin) * scale).astype(BF16)
        out_ref[0, :, out_col + half:out_col + RET_HEAD_QK] = (
            (t2 * cos + t1 * sin) * scale).astype(BF16)

    for h in range(RET_HEADS):
        rotary(h * RET_HEAD_QK, 1.0, q_ref, h * RET_HEAD_QK)
        rotary(RET_QK_DIM + h * RET_HEAD_QK, RET_HEAD_QK ** -0.5, k_ref, h * RET_HEAD_QK)
    v0 = 2 * RET_QK_DIM
    g0 = v0 + RET_V_DIM
    for h in range(RET_HEADS):
        lo = h * RET_HEAD_V
        v = jnp.dot(hb_ref[...], win_ref[:, v0 + lo:v0 + lo + RET_HEAD_V],
                    preferred_element_type=F32)
        v_ref[0, :, lo:lo + RET_HEAD_V] = v.astype(BF16)
        gt = jnp.dot(hb_ref[...], win_ref[:, g0 + lo:g0 + lo + RET_HEAD_V],
                     preferred_element_type=F32)
        sg_ref[0, :, lo:lo + RET_HEAD_V] = (gt * _sigmoid(gt)).astype(BF16)


def _ret_proj(x, g, w_in, cos, sin, *, tm=TOKEN_TILE):
    b, s, d = x.shape
    half = RET_HEAD_QK // 2
    width = 2 * RET_QK_DIM + 2 * RET_V_DIM
    return pl.pallas_call(
        _ret_proj_kernel,
        grid=(b, s // tm),
        in_specs=[
            _token_spec(tm, d),
            _resident((1, d)),
            _resident((d, width)),
            _token_spec(tm, half),
            _token_spec(tm, half),
        ],
        out_specs=[
            _token_spec(tm, RET_QK_DIM),
            _token_spec(tm, RET_QK_DIM),
            _token_spec(tm, RET_V_DIM),
            _token_spec(tm, RET_V_DIM),
        ],
        out_shape=[
            jax.ShapeDtypeStruct((b, s, RET_QK_DIM), BF16),
            jax.ShapeDtypeStruct((b, s, RET_QK_DIM), BF16),
            jax.ShapeDtypeStruct((b, s, RET_V_DIM), BF16),
            jax.ShapeDtypeStruct((b, s, RET_V_DIM), BF16),
        ],
        scratch_shapes=[pltpu.VMEM((tm, d), BF16)],
        compiler_params=_compiler_params(),
        name="ret_proj",
    )(x, g.reshape(1, d), w_in.astype(BF16), cos, sin)


def _ret_scan_kernel(x_ref, q_ref, k_ref, v_ref, sg_ref, intra_ref, qd_ref, kd_ref, cd_ref,
                     gng_ref, wout_ref, o_ref, state_ref, og_ref, *, tm):
    @pl.when(pl.program_id(1) == 0)
    def _():
        state_ref[...] = jnp.zeros_like(state_ref)

    c = RET_CHUNK
    for h in range(RET_HEADS):
        qk = slice(h * RET_HEAD_QK, (h + 1) * RET_HEAD_QK)
        vv = slice(h * RET_HEAD_V, (h + 1) * RET_HEAD_V)
        qd = qd_ref[h]
        kd = kd_ref[h]
        for n in range(tm // c):
            rows = slice(n * c, (n + 1) * c)
            qi = q_ref[0, rows, qk]
            ki = k_ref[0, rows, qk]
            vi = v_ref[0, rows, vv]
            scores = lax.dot_general(qi, ki, (((1,), (1,)), ((), ())),
                                     preferred_element_type=F32) * intra_ref[h]
            inner = jnp.dot(scores.astype(BF16), vi, preferred_element_type=F32)
            state = state_ref[h]
            cross = jnp.dot(qi, state.astype(BF16), preferred_element_type=F32) * qd
            kdec = (ki.astype(F32) * kd).astype(BF16)
            kv = lax.dot_general(kdec, vi, (((0,), (0,)), ((), ())),
                                 preferred_element_type=F32)
            state_ref[h] = cd_ref[h] * state + kv
            on = _center_scale(inner + cross) * gng_ref[:, vv]
            og_ref[rows, vv] = (sg_ref[0, rows, vv].astype(F32) * on).astype(BF16)

    o_ref[0] = x_ref[0] + jnp.dot(og_ref[...], wout_ref[...], preferred_element_type=F32)


def _decay_tables():
    c = RET_CHUNK
    log_gamma = jnp.log1p(-jnp.exp2(-5.0 - jnp.arange(RET_HEADS, dtype=F32)))
    idx = jnp.arange(c, dtype=F32)
    dist = idx[:, None] - idx[None, :]
    intra = jnp.where(dist[None] >= 0,
                      jnp.exp(log_gamma[:, None, None] * jnp.maximum(dist, 0.0)[None]), 0.0)
    q_decay = jnp.exp(log_gamma[:, None] * (idx + 1.0)[None])
    k_decay = jnp.exp(log_gamma[:, None] * (c - 1.0 - idx)[None])
    chunk_decay = jnp.exp(log_gamma * c)
    return (intra,
            jnp.broadcast_to(q_decay[:, :, None], (RET_HEADS, c, RET_HEAD_V)),
            jnp.broadcast_to(k_decay[:, :, None], (RET_HEADS, c, RET_HEAD_QK)),
            jnp.broadcast_to(chunk_decay[:, None, None], (RET_HEADS, 1, RET_HEAD_V)))


def _ret_scan(x, q, k, v, sg, gn_g, w_out, *, tm=TOKEN_TILE):
    b, s, d = x.shape
    intra, qd, kd, cd = _decay_tables()
    return pl.pallas_call(
        functools.partial(_ret_scan_kernel, tm=tm),
        grid=(b, s // tm),
        in_specs=[
            _token_spec(tm, d),
            _token_spec(tm, RET_QK_DIM),
            _token_spec(tm, RET_QK_DIM),
            _token_spec(tm, RET_V_DIM),
            _token_spec(tm, RET_V_DIM),
            _resident(intra.shape),
            _resident(qd.shape),
            _resident(kd.shape),
            _resident(cd.shape),
            _resident((1, RET_V_DIM)),
            _resident((RET_V_DIM, d)),
        ],
        out_specs=_token_spec(tm, d),
        out_shape=jax.ShapeDtypeStruct(x.shape, x.dtype),
        scratch_shapes=[
            pltpu.VMEM((RET_HEADS, RET_HEAD_QK, RET_HEAD_V), F32),
            pltpu.VMEM((tm, RET_V_DIM), BF16),
        ],
        compiler_params=_compiler_params(),
        name="ret_scan",
    )(x, q, k, v, sg, intra, qd, kd, cd, gn_g.reshape(1, RET_V_DIM), w_out.astype(BF16))


def _final_norm_kernel(x_ref, g_ref, o_ref):
    o_ref[0] = _rmsnorm(x_ref[0], g_ref[...])


def _final_norm(x, g, *, tm=TOKEN_TILE):
    b, s, d = x.shape
    return pl.pallas_call(
        _final_norm_kernel,
        grid=(b, s // tm),
        in_specs=[_token_spec(tm, d), _resident((1, d))],
        out_specs=_token_spec(tm, d),
        out_shape=jax.ShapeDtypeStruct(x.shape, x.dtype),
        compiler_params=_compiler_params(),
        name="final_norm",
    )(x, g.reshape(1, d))


def kernel(x, positions, conv_w_in, conv_dw_w, conv_dw_b, conv_ln_g, conv_ln_b, conv_w_out,
           ret_w_in, ret_gn_g, ret_w_out, ffn_w_in, ffn_dw_w, ffn_dw_b, ffn_w_out,
           norm_mix_g, norm_ffn_g, final_g):
    depth = norm_mix_g.shape[0]
    cos, sin = _rope_tables(positions)
    for i in range(depth):
        j = i // 2
        if i % 2 == 0:
            x = _conv_module(x, norm_mix_g[i], conv_w_in[j], conv_dw_w[j], conv_dw_b[j],
                             conv_ln_g[j], conv_ln_b[j], conv_w_out[j])
        else:
            q, k, v, sg = _ret_proj(x, norm_mix_g[i], ret_w_in[j], cos, sin)
            x = _ret_scan(x, q, k, v, sg, ret_gn_g[j], ret_w_out[j])
        x = _conv_ffn(x, norm_ffn_g[i], ffn_w_in[i], ffn_dw_w[i], ffn_dw_b[i], ffn_w_out[i])
    return _final_norm(x, final_g)
```

```python
import functools

import jax
import jax.numpy as jnp
from jax import lax
from jax.experimental import pallas as pl
from jax.experimental.pallas import tpu as pltpu

D_MODEL = 1024
CONV_WIDTH = 31
RET_HEADS = 4
RET_HEAD_QK = 256
RET_HEAD_V = 512
RET_QK_DIM = RET_HEADS * RET_HEAD_QK
RET_V_DIM = RET_HEADS * RET_HEAD_V
RET_CHUNK = 128
ROPE_BASE = 10000.0
FFN_DIM = 2816
FFN_CONV_WIDTH = 3
EPS = 1e-6

F32 = jnp.float32
BF16 = jnp.bfloat16

V7X_VMEM_BYTES = 64 * 1024 * 1024
VMEM_LIMIT_BYTES = V7X_VMEM_BYTES - 8 * 1024 * 1024
SUBLANES = 8
LANES = 128

TOKEN_TILE = 512
FFN_COL_TILE = 256
CONV_HIST = 32
CONV_ROW_BLOCK = 64


def _resident(shape):
    zeros = (0,) * len(shape)
    return pl.BlockSpec(shape, lambda *_: zeros, pipeline_mode=pl.Buffered(1))


def _layer_resident(shape, layer):
    index = (layer,) + (0,) * len(shape)
    return pl.BlockSpec((None,) + tuple(shape), lambda *_: index, pipeline_mode=pl.Buffered(1))


def _rows(p):
    return p.reshape(p.shape[0], 1, p.shape[1])


def _token_spec(tm, width):
    return pl.BlockSpec((1, tm, width), lambda b, i: (b, i, 0))


def _compiler_params():
    return pltpu.CompilerParams(
        dimension_semantics=("arbitrary", "arbitrary"),
        vmem_limit_bytes=VMEM_LIMIT_BYTES,
    )


def _rmsnorm(x, g):
    ms = jnp.mean(x * x, axis=-1, keepdims=True)
    return x * lax.rsqrt(ms + EPS) * g


def _center_scale(x):
    mu = jnp.mean(x, axis=-1, keepdims=True)
    xc = x - mu
    var = jnp.mean(xc * xc, axis=-1, keepdims=True)
    return xc * lax.rsqrt(var + EPS)


def _sigmoid(x):
    return 1.0 / (1.0 + jnp.exp(-x))


def _shift_rows(a, prev, k, row):
    out = pltpu.roll(a, k, 0)
    for r in range(k):
        out = jnp.where(row == r, prev[SUBLANES - k + r:SUBLANES - k + r + 1, :], out)
    return out


def _ffn_kernel(x_ref, g_ref, win_ref, dww_ref, dwb_ref, wout_ref, fg_ref, o_ref,
                hb_ref, act_ref, carry_ref, *, tm, fc, final_norm):
    x = x_ref[0]
    hb_ref[...] = _rmsnorm(x, g_ref[...]).astype(BF16)

    @pl.when(pl.program_id(1) == 0)
    def _():
        carry_ref[...] = jnp.zeros_like(carry_ref)

    row = lax.broadcasted_iota(jnp.int32, (tm, fc), 0)
    for c in range(FFN_DIM // fc):
        lo = c * fc
        hb = hb_ref[...]
        a = jnp.dot(hb, win_ref[:, lo:lo + fc], preferred_element_type=F32)
        u = jnp.dot(hb, win_ref[:, FFN_DIM + lo:FFN_DIM + lo + fc],
                    preferred_element_type=F32)
        prev = carry_ref[:, lo:lo + fc]
        carry_ref[:, lo:lo + fc] = a[tm - SUBLANES:, :]
        a1 = _shift_rows(a, prev, 1, row)
        a2 = _shift_rows(a, prev, 2, row)
        conv = (dww_ref[2:3, lo:lo + fc] * a + dww_ref[1:2, lo:lo + fc] * a1
                + dww_ref[0:1, lo:lo + fc] * a2 + dwb_ref[:, lo:lo + fc])
        act_ref[:, lo:lo + fc] = (conv * _sigmoid(conv) * u).astype(BF16)

    y = x + jnp.dot(act_ref[...], wout_ref[...], preferred_element_type=F32)
    o_ref[0] = _rmsnorm(y, fg_ref[...]) if final_norm else y


def _conv_ffn(x, layer, g, w_in, dw_w, dw_b, w_out, final_g, *, final_norm,
              tm=TOKEN_TILE, fc=FFN_COL_TILE):
    b, s, d = x.shape
    f = FFN_DIM
    return pl.pallas_call(
        functools.partial(_ffn_kernel, tm=tm, fc=fc, final_norm=final_norm),
        grid=(b, s // tm),
        in_specs=[
            _token_spec(tm, d),
            _layer_resident((1, d), layer),
            _layer_resident((d, 2 * f), layer),
            _layer_resident((FFN_CONV_WIDTH, f), layer),
            _layer_resident((1, f), layer),
            _layer_resident((f, d), layer),
            _resident((1, d)),
        ],
        out_specs=_token_spec(tm, d),
        out_shape=jax.ShapeDtypeStruct(x.shape, x.dtype),
        scratch_shapes=[
            pltpu.VMEM((tm, d), BF16),
            pltpu.VMEM((tm, f), BF16),
            pltpu.VMEM((SUBLANES, f), F32),
        ],
        compiler_params=_compiler_params(),
        name="conv_ffn",
    )(x, g, w_in, dw_w, dw_b, w_out, final_g)


def _convmod_kernel(x_ref, g_ref, win_ref, dww_ref, dwb_ref, lng_ref, lnb_ref, wout_ref,
                    o_ref, buf_ref, y_ref, *, tm):
    d = D_MODEL
    x = x_ref[0]
    hb = _rmsnorm(x, g_ref[...]).astype(BF16)

    @pl.when(pl.program_id(1) == 0)
    def _():
        buf_ref[0:CONV_HIST, :] = jnp.zeros((CONV_HIST, d), F32)

    @pl.when(pl.program_id(1) > 0)
    def _():
        buf_ref[0:CONV_HIST, :] = buf_ref[tm:tm + CONV_HIST, :]

    a = jnp.dot(hb, win_ref[:, :d], preferred_element_type=F32)
    gate = jnp.dot(hb, win_ref[:, d:], preferred_element_type=F32)
    buf_ref[CONV_HIST:CONV_HIST + tm, :] = a * _sigmoid(gate)

    base = CONV_HIST - (CONV_WIDTH - 1)
    rb = CONV_ROW_BLOCK
    for r0 in range(0, tm, rb):
        for c0 in range(0, d, LANES):
            total = None
            for phase in range(SUBLANES):
                part = None
                for j in range(CONV_WIDTH):
                    off = base + j
                    if off % SUBLANES != phase:
                        continue
                    term = (dww_ref[j:j + 1, c0:c0 + LANES]
                            * buf_ref[r0 + off:r0 + off + rb, c0:c0 + LANES])
                    part = term if part is None else part + term
                if part is not None:
                    total = part if total is None else total + part
            y_ref[r0:r0 + rb, c0:c0 + LANES] = total + dwb_ref[:, c0:c0 + LANES]

    yn = _center_scale(y_ref[...]) * lng_ref[...] + lnb_ref[...]
    act = (yn * _sigmoid(yn)).astype(BF16)
    o_ref[0] = x + jnp.dot(act, wout_ref[...], preferred_element_type=F32)


def _conv_module(x, layer, g, w_in, dw_w, dw_b, ln_g, ln_b, w_out, *, tm=TOKEN_TILE):
    b, s, d = x.shape
    return pl.pallas_call(
        functools.partial(_convmod_kernel, tm=tm),
        grid=(b, s // tm),
        in_specs=[
            _token_spec(tm, d),
            _layer_resident((1, d), layer),
            _layer_resident((d, 2 * d), layer // 2),
            _layer_resident((CONV_WIDTH, d), layer // 2),
            _layer_resident((1, d), layer // 2),
            _layer_resident((1, d), layer // 2),
            _layer_resident((1, d), layer // 2),
            _layer_resident((d, d), layer // 2),
        ],
        out_specs=_token_spec(tm, d),
        out_shape=jax.ShapeDtypeStruct(x.shape, x.dtype),
        scratch_shapes=[
            pltpu.VMEM((CONV_HIST + tm, d), F32),
            pltpu.VMEM((tm, d), F32),
        ],
        compiler_params=_compiler_params(),
        name="conv_module",
    )(x, g, w_in, dw_w, dw_b, ln_g, ln_b, w_out)


def _rope_kernel(pos_ref, invf_ref, cos_ref, sin_ref):
    ang = pos_ref[0] * invf_ref[...]
    cos_ref[0] = jnp.cos(ang)
    sin_ref[0] = jnp.sin(ang)


def _rope_tables(positions, *, tm=TOKEN_TILE):
    b, s = positions.shape
    half = RET_HEAD_QK // 2
    inv_freq = 1.0 / (ROPE_BASE ** (jnp.arange(0, RET_HEAD_QK, 2, dtype=F32) / RET_HEAD_QK))
    table = jax.ShapeDtypeStruct((b, s, half), F32)
    return pl.pallas_call(
        _rope_kernel,
        grid=(b, s // tm),
        in_specs=[_token_spec(tm, 1), _resident((1, half))],
        out_specs=[_token_spec(tm, half), _token_spec(tm, half)],
        out_shape=[table, table],
        compiler_params=_compiler_params(),
        name="rope_tables",
    )(positions.astype(F32).reshape(b, s, 1), inv_freq.reshape(1, half))


def _ret_proj_kernel(x_ref, g_ref, win_ref, cos_ref, sin_ref,
                     q_ref, k_ref, v_ref, sg_ref, hb_ref):
    hb_ref[...] = _rmsnorm(x_ref[0], g_ref[...]).astype(BF16)
    half = RET_HEAD_QK // 2
    cos = cos_ref[0]
    sin = sin_ref[0]
    v0 = 2 * RET_QK_DIM
    g0 = v0 + RET_V_DIM
    for h in range(RET_HEADS):
        lo = h * RET_HEAD_V
        v = jnp.dot(hb_ref[...], win_ref[:, v0 + lo:v0 + lo + RET_HEAD_V],
                    preferred_element_type=F32)
        v_ref[0, :, lo:lo + RET_HEAD_V] = v.astype(BF16)
        gt = jnp.dot(hb_ref[...], win_ref[:, g0 + lo:g0 + lo + RET_HEAD_V],
                     preferred_element_type=F32)
        sg_ref[0, :, lo:lo + RET_HEAD_V] = (gt * _sigmoid(gt)).astype(BF16)

    def rotary(col, scale, out_ref, out_col):
        t = jnp.dot(hb_ref[...], win_ref[:, col:col + RET_HEAD_QK], preferred_element_type=F32)
        t1, t2 = t[:, :half], t[:, half:]
        out_ref[0, :, out_col:out_col + half] = ((t1 * cos - t2 * sin) * scale).astype(BF16)
        out_ref[0, :, out_col + half:out_col + RET_HEAD_QK] = (
            (t2 * cos + t1 * sin) * scale).astype(BF16)

    for h in range(RET_HEADS):
        rotary(h * RET_HEAD_QK, 1.0, q_ref, h * RET_HEAD_QK)
        rotary(RET_QK_DIM + h * RET_HEAD_QK, RET_HEAD_QK ** -0.5, k_ref, h * RET_HEAD_QK)


def _ret_proj(x, layer, g, w_in, cos, sin, *, tm=TOKEN_TILE):
    b, s, d = x.shape
    half = RET_HEAD_QK // 2
    width = 2 * RET_QK_DIM + 2 * RET_V_DIM
    return pl.pallas_call(
        _ret_proj_kernel,
        grid=(b, s // tm),
        in_specs=[
            _token_spec(tm, d),
            _layer_resident((1, d), layer),
            _layer_resident((d, width), layer // 2),
            _token_spec(tm, half),
            _token_spec(tm, half),
        ],
        out_specs=[
            _token_spec(tm, RET_QK_DIM),
            _token_spec(tm, RET_QK_DIM),
            _token_spec(tm, RET_V_DIM),
            _token_spec(tm, RET_V_DIM),
        ],
        out_shape=[
            jax.ShapeDtypeStruct((b, s, RET_QK_DIM), BF16),
            jax.ShapeDtypeStruct((b, s, RET_QK_DIM), BF16),
            jax.ShapeDtypeStruct((b, s, RET_V_DIM), BF16),
            jax.ShapeDtypeStruct((b, s, RET_V_DIM), BF16),
        ],
        scratch_shapes=[pltpu.VMEM((tm, d), BF16)],
        compiler_params=_compiler_params(),
        name="ret_proj",
    )(x, g, w_in, cos, sin)


def _ret_scan_kernel(x_ref, q_ref, k_ref, v_ref, sg_ref, intra_ref, qd_ref, kd_ref, cd_ref,
                     gng_ref, wout_ref, o_ref, state_ref, og_ref, *, tm):
    @pl.when(pl.program_id(1) == 0)
    def _():
        state_ref[...] = jnp.zeros_like(state_ref)

    c = RET_CHUNK
    for h in range(RET_HEADS):
        qk = slice(h * RET_HEAD_QK, (h + 1) * RET_HEAD_QK)
        vv = slice(h * RET_HEAD_V, (h + 1) * RET_HEAD_V)
        qd = qd_ref[h]
        kd = kd_ref[h]
        for n in range(tm // c):
            rows = slice(n * c, (n + 1) * c)
            qi = q_ref[0, rows, qk]
            ki = k_ref[0, rows, qk]
            vi = v_ref[0, rows, vv]
            scores = lax.dot_general(qi, ki, (((1,), (1,)), ((), ())),
                                     preferred_element_type=F32) * intra_ref[h]
            inner = jnp.dot(scores.astype(BF16), vi, preferred_element_type=F32)
            state = state_ref[h]
            cross = jnp.dot(qi, state.astype(BF16), preferred_element_type=F32) * qd
            kdec = (ki.astype(F32) * kd).astype(BF16)
            kv = lax.dot_general(kdec, vi, (((0,), (0,)), ((), ())),
                                 preferred_element_type=F32)
            state_ref[h] = cd_ref[h] * state + kv
            on = _center_scale(inner + cross) * gng_ref[:, vv]
            og_ref[rows, vv] = (sg_ref[0, rows, vv].astype(F32) * on).astype(BF16)

    o_ref[0] = x_ref[0] + jnp.dot(og_ref[...], wout_ref[...], preferred_element_type=F32)


def _decay_tables():
    c = RET_CHUNK
    log_gamma = jnp.log1p(-jnp.exp2(-5.0 - jnp.arange(RET_HEADS, dtype=F32)))
    idx = jnp.arange(c, dtype=F32)
    dist = idx[:, None] - idx[None, :]
    intra = jnp.where(dist[None] >= 0,
                      jnp.exp(log_gamma[:, None, None] * jnp.maximum(dist, 0.0)[None]), 0.0)
    q_decay = jnp.exp(log_gamma[:, None] * (idx + 1.0)[None])
    k_decay = jnp.exp(log_gamma[:, None] * (c - 1.0 - idx)[None])
    chunk_decay = jnp.exp(log_gamma * c)
    return (intra,
            jnp.broadcast_to(q_decay[:, :, None], (RET_HEADS, c, RET_HEAD_V)),
            jnp.broadcast_to(k_decay[:, :, None], (RET_HEADS, c, RET_HEAD_QK)),
            jnp.broadcast_to(chunk_decay[:, None, None], (RET_HEADS, 1, RET_HEAD_V)))


def _ret_scan(x, layer, q, k, v, sg, gn_g, w_out, *, tm=TOKEN_TILE):
    b, s, d = x.shape
    intra, qd, kd, cd = _decay_tables()
    return pl.pallas_call(
        functools.partial(_ret_scan_kernel, tm=tm),
        grid=(b, s // tm),
        in_specs=[
            _token_spec(tm, d),
            _token_spec(tm, RET_QK_DIM),
            _token_spec(tm, RET_QK_DIM),
            _token_spec(tm, RET_V_DIM),
            _token_spec(tm, RET_V_DIM),
            _resident(intra.shape),
            _resident(qd.shape),
            _resident(kd.shape),
            _resident(cd.shape),
            _layer_resident((1, RET_V_DIM), layer // 2),
            _layer_resident((RET_V_DIM, d), layer // 2),
        ],
        out_specs=_token_spec(tm, d),
        out_shape=jax.ShapeDtypeStruct(x.shape, x.dtype),
        scratch_shapes=[
            pltpu.VMEM((RET_HEADS, RET_HEAD_QK, RET_HEAD_V), F32),
            pltpu.VMEM((tm, RET_V_DIM), BF16),
        ],
        compiler_params=_compiler_params(),
        name="ret_scan",
    )(x, q, k, v, sg, intra, qd, kd, cd, gn_g, w_out)


def kernel(x, positions, conv_w_in, conv_dw_w, conv_dw_b, conv_ln_g, conv_ln_b, conv_w_out,
           ret_w_in, ret_gn_g, ret_w_out, ffn_w_in, ffn_dw_w, ffn_dw_b, ffn_w_out,
           norm_mix_g, norm_ffn_g, final_g):
    depth = norm_mix_g.shape[0]
    d = x.shape[-1]
    conv_w_in, conv_w_out = conv_w_in.astype(BF16), conv_w_out.astype(BF16)
    ret_w_in, ret_w_out = ret_w_in.astype(BF16), ret_w_out.astype(BF16)
    ffn_w_in, ffn_w_out = ffn_w_in.astype(BF16), ffn_w_out.astype(BF16)
    conv_dw_b, conv_ln_g, conv_ln_b = _rows(conv_dw_b), _rows(conv_ln_g), _rows(conv_ln_b)
    ret_gn_g, ffn_dw_b = _rows(ret_gn_g), _rows(ffn_dw_b)
    norm_mix_g, norm_ffn_g = _rows(norm_mix_g), _rows(norm_ffn_g)
    final_g = final_g.reshape(1, d)
    cos, sin = _rope_tables(positions)

    for i in range(depth):
        if i % 2 == 0:
            x = _conv_module(x, i, norm_mix_g, conv_w_in, conv_dw_w, conv_dw_b,
                             conv_ln_g, conv_ln_b, conv_w_out)
        else:
            q, k, v, sg = _ret_proj(x, i, norm_mix_g, ret_w_in, cos, sin)
            x = _ret_scan(x, i, q, k, v, sg, ret_gn_g, ret_w_out)
        x = _conv_ffn(x, i, norm_ffn_g, ffn_w_in, ffn_dw_w, ffn_dw_b, ffn_w_out, final_g,
                      final_norm=(i == depth - 1))
    return x
```

```python
import functools

import jax
import jax.numpy as jnp
from jax import lax
from jax.experimental import pallas as pl
from jax.experimental.pallas import tpu as pltpu

D_MODEL = 1024
CONV_WIDTH = 31
RET_HEADS = 4
RET_HEAD_QK = 256
RET_HEAD_V = 512
RET_QK_DIM = RET_HEADS * RET_HEAD_QK
RET_V_DIM = RET_HEADS * RET_HEAD_V
RET_CHUNK = 128
ROPE_BASE = 10000.0
FFN_DIM = 2816
FFN_CONV_WIDTH = 3
EPS = 1e-6

F32 = jnp.float32
BF16 = jnp.bfloat16

V7X_VMEM_BYTES = 64 * 1024 * 1024
VMEM_LIMIT_BYTES = V7X_VMEM_BYTES - 8 * 1024 * 1024
SUBLANES = 8
LANES = 128

TOKEN_TILE = 512
FFN_COL_TILE = 256
CONV_HIST = 32
CONV_ROW_BLOCK = 64
CONV_COL_TILE = 256


def _resident(shape):
    zeros = (0,) * len(shape)
    return pl.BlockSpec(shape, lambda *_: zeros, pipeline_mode=pl.Buffered(1))


def _layer_resident(shape, layer):
    index = (layer,) + (0,) * len(shape)
    return pl.BlockSpec((None,) + tuple(shape), lambda *_: index, pipeline_mode=pl.Buffered(1))


def _rows(p):
    return p.reshape(p.shape[0], 1, p.shape[1])


def _token_spec(tm, width):
    return pl.BlockSpec((1, tm, width), lambda b, i: (b, i, 0))


def _compiler_params():
    return pltpu.CompilerParams(
        dimension_semantics=("arbitrary", "arbitrary"),
        vmem_limit_bytes=VMEM_LIMIT_BYTES,
    )


def _rmsnorm(x, g):
    ms = jnp.mean(x * x, axis=-1, keepdims=True)
    return x * lax.rsqrt(ms + EPS) * g


def _center_scale(x):
    mu = jnp.mean(x, axis=-1, keepdims=True)
    xc = x - mu
    var = jnp.mean(xc * xc, axis=-1, keepdims=True)
    return xc * lax.rsqrt(var + EPS)


def _sigmoid(x):
    return 1.0 / (1.0 + jnp.exp(-x))


def _shift_rows(a, prev, k, row):
    out = pltpu.roll(a, k, 0)
    for r in range(k):
        out = jnp.where(row == r, prev[SUBLANES - k + r:SUBLANES - k + r + 1, :], out)
    return out


def _ffn_hidden(hb_ref, win_ref, dww_ref, dwb_ref, act_ref, carry_ref, *, tm, fc,
                after_chunk=None):
    row = lax.broadcasted_iota(jnp.int32, (tm, fc), 0)
    for c in range(FFN_DIM // fc):
        lo = c * fc
        hb = hb_ref[...]
        a = jnp.dot(hb, win_ref[:, lo:lo + fc], preferred_element_type=F32)
        u = jnp.dot(hb, win_ref[:, FFN_DIM + lo:FFN_DIM + lo + fc],
                    preferred_element_type=F32)
        prev = carry_ref[:, lo:lo + fc]
        carry_ref[:, lo:lo + fc] = a[tm - SUBLANES:, :]
        a1 = _shift_rows(a, prev, 1, row)
        a2 = _shift_rows(a, prev, 2, row)
        conv = (dww_ref[2:3, lo:lo + fc] * a + dww_ref[1:2, lo:lo + fc] * a1
                + dww_ref[0:1, lo:lo + fc] * a2 + dwb_ref[:, lo:lo + fc])
        act_ref[:, lo:lo + fc] = (conv * _sigmoid(conv) * u).astype(BF16)
        if after_chunk is not None:
            after_chunk(c)


def _ffn_kernel(x_ref, g_ref, win_ref, dww_ref, dwb_ref, wout_ref, fg_ref, o_ref,
                hb_ref, act_ref, carry_ref, *, tm, fc, final_norm):
    x = x_ref[0]
    hb_ref[...] = _rmsnorm(x, g_ref[...]).astype(BF16)

    @pl.when(pl.program_id(1) == 0)
    def _():
        carry_ref[...] = jnp.zeros_like(carry_ref)

    _ffn_hidden(hb_ref, win_ref, dww_ref, dwb_ref, act_ref, carry_ref, tm=tm, fc=fc)
    y = x + jnp.dot(act_ref[...], wout_ref[...], preferred_element_type=F32)
    o_ref[0] = _rmsnorm(y, fg_ref[...]) if final_norm else y


def _conv_ffn(x, layer, g, w_in, dw_w, dw_b, w_out, final_g, *, final_norm,
              tm=TOKEN_TILE, fc=FFN_COL_TILE):
    b, s, d = x.shape
    f = FFN_DIM
    return pl.pallas_call(
        functools.partial(_ffn_kernel, tm=tm, fc=fc, final_norm=final_norm),
        grid=(b, s // tm),
        in_specs=[
            _token_spec(tm, d),
            _layer_resident((1, d), layer),
            _layer_resident((d, 2 * f), layer),
            _layer_resident((FFN_CONV_WIDTH, f), layer),
            _layer_resident((1, f), layer),
            _layer_resident((f, d), layer),
            _resident((1, d)),
        ],
        out_specs=_token_spec(tm, d),
        out_shape=jax.ShapeDtypeStruct(x.shape, x.dtype),
        scratch_shapes=[
            pltpu.VMEM((tm, d), BF16),
            pltpu.VMEM((tm, f), BF16),
            pltpu.VMEM((SUBLANES, f), F32),
        ],
        compiler_params=_compiler_params(),
        name="conv_ffn",
    )(x, g, w_in, dw_w, dw_b, w_out, final_g)


def _conv_module_jobs(win_ref, dww_ref, dwb_ref, hb_ref, buf_ref, y_ref, *, tm):
    d = D_MODEL
    base = CONV_HIST - (CONV_WIDTH - 1)
    rb = CONV_ROW_BLOCK
    cw = CONV_COL_TILE

    def glu(cb):
        hb = hb_ref[...]
        a = jnp.dot(hb, win_ref[:, cb:cb + cw], preferred_element_type=F32)
        gate = jnp.dot(hb, win_ref[:, d + cb:d + cb + cw], preferred_element_type=F32)
        buf_ref[CONV_HIST:CONV_HIST + tm, cb:cb + cw] = a * _sigmoid(gate)

    def conv_block(r0, c0):
        total = None
        for phase in range(SUBLANES):
            part = None
            for j in range(CONV_WIDTH):
                off = base + j
                if off % SUBLANES != phase:
                    continue
                term = (dww_ref[j:j + 1, c0:c0 + LANES]
                        * buf_ref[r0 + off:r0 + off + rb, c0:c0 + LANES])
                part = term if part is None else part + term
            if part is not None:
                total = part if total is None else total + part
        y_ref[r0:r0 + rb, c0:c0 + LANES] = total + dwb_ref[:, c0:c0 + LANES]

    jobs = []
    for cb in range(0, d, cw):
        jobs.append(functools.partial(glu, cb))
        for c0 in range(cb, cb + cw, LANES):
            for r0 in range(0, tm, rb):
                jobs.append(functools.partial(conv_block, r0, c0))
    return jobs


def _conv_ffn_fused_kernel(x_ref, gm_ref, cwin_ref, cdww_ref, cdwb_ref, lng_ref, lnb_ref,
                           cwout_ref, gf_ref, fwin_ref, fdww_ref, fdwb_ref, fwout_ref, fg_ref,
                           o_ref, x1_ref, hbc_ref, buf_ref, y_ref, hb_ref, act_ref, carry_ref,
                           *, tm, fc, nt, final_norm):
    d = D_MODEL
    n = pl.program_id(0)
    conv_seq_start = n % nt == 0
    ffn_seq_start = (n - 1) % nt == 0

    @pl.when(n == 0)
    def _():
        x1_ref[...] = jnp.zeros_like(x1_ref)

    @pl.when(conv_seq_start)
    def _():
        buf_ref[0:CONV_HIST, :] = jnp.zeros((CONV_HIST, d), F32)

    @pl.when(jnp.logical_not(conv_seq_start))
    def _():
        buf_ref[0:CONV_HIST, :] = buf_ref[tm:tm + CONV_HIST, :]

    @pl.when(jnp.logical_or(ffn_seq_start, n == 0))
    def _():
        carry_ref[...] = jnp.zeros_like(carry_ref)

    x1 = x1_ref[...]
    hb_ref[...] = _rmsnorm(x1, gf_ref[...]).astype(BF16)
    x = x_ref[0]
    hbc_ref[...] = _rmsnorm(x, gm_ref[...]).astype(BF16)

    jobs = _conv_module_jobs(cwin_ref, cdww_ref, cdwb_ref, hbc_ref, buf_ref, y_ref, tm=tm)
    per_chunk = -(-len(jobs) // (FFN_DIM // fc))

    def conv_jobs_after_chunk(c):
        for job in jobs[c * per_chunk:(c + 1) * per_chunk]:
            job()

    _ffn_hidden(hb_ref, fwin_ref, fdww_ref, fdwb_ref, act_ref, carry_ref, tm=tm, fc=fc,
                after_chunk=conv_jobs_after_chunk)
    y = x1 + jnp.dot(act_ref[...], fwout_ref[...], preferred_element_type=F32)
    o_ref[0] = _rmsnorm(y, fg_ref[...]) if final_norm else y

    yn = _center_scale(y_ref[...]) * lng_ref[...] + lnb_ref[...]
    act = (yn * _sigmoid(yn)).astype(BF16)
    x1_ref[...] = x + jnp.dot(act, cwout_ref[...], preferred_element_type=F32)


def _conv_module_ffn(x, layer, mix_g, c_w_in, c_dw_w, c_dw_b, ln_g, ln_b, c_w_out,
                     ffn_g, f_w_in, f_dw_w, f_dw_b, f_w_out, final_g, *, final_norm,
                     tm=TOKEN_TILE, fc=FFN_COL_TILE):
    b, s, d = x.shape
    f = FFN_DIM
    nt = s // tm
    ntiles = b * nt

    def conv_tile(n):
        m = jnp.minimum(n, ntiles - 1)
        return (m // nt, m % nt, 0)

    def ffn_tile(n):
        m = jnp.maximum(n - 1, 0)
        return (m // nt, m % nt, 0)

    return pl.pallas_call(
        functools.partial(_conv_ffn_fused_kernel, tm=tm, fc=fc, nt=nt, final_norm=final_norm),
        grid=(ntiles + 1,),
        in_specs=[
            pl.BlockSpec((1, tm, d), conv_tile),
            _layer_resident((1, d), layer),
            _layer_resident((d, 2 * d), layer // 2),
            _layer_resident((CONV_WIDTH, d), layer // 2),
            _layer_resident((1, d), layer // 2),
            _layer_resident((1, d), layer // 2),
            _layer_resident((1, d), layer // 2),
            _layer_resident((d, d), layer // 2),
            _layer_resident((1, d), layer),
            _layer_resident((d, 2 * f), layer),
            _layer_resident((FFN_CONV_WIDTH, f), layer),
            _layer_resident((1, f), layer),
            _layer_resident((f, d), layer),
            _resident((1, d)),
        ],
        out_specs=pl.BlockSpec((1, tm, d), ffn_tile),
        out_shape=jax.ShapeDtypeStruct(x.shape, x.dtype),
        scratch_shapes=[
            pltpu.VMEM((tm, d), F32),
            pltpu.VMEM((tm, d), BF16),
            pltpu.VMEM((CONV_HIST + tm, d), F32),
            pltpu.VMEM((tm, d), F32),
            pltpu.VMEM((tm, d), BF16),
            pltpu.VMEM((tm, f), BF16),
            pltpu.VMEM((SUBLANES, f), F32),
        ],
        compiler_params=pltpu.CompilerParams(
            dimension_semantics=("arbitrary",),
            vmem_limit_bytes=VMEM_LIMIT_BYTES,
        ),
        name="conv_module_ffn",
    )(x, mix_g, c_w_in, c_dw_w, c_dw_b, ln_g, ln_b, c_w_out,
      ffn_g, f_w_in, f_dw_w, f_dw_b, f_w_out, final_g)


def _rope_kernel(pos_ref, invf_ref, cos_ref, sin_ref):
    ang = pos_ref[0] * invf_ref[...]
    cos_ref[0] = jnp.cos(ang)
    sin_ref[0] = jnp.sin(ang)


def _rope_tables(positions, *, tm=TOKEN_TILE):
    b, s = positions.shape
    half = RET_HEAD_QK // 2
    inv_freq = 1.0 / (ROPE_BASE ** (jnp.arange(0, RET_HEAD_QK, 2, dtype=F32) / RET_HEAD_QK))
    table = jax.ShapeDtypeStruct((b, s, half), F32)
    return pl.pallas_call(
        _rope_kernel,
        grid=(b, s // tm),
        in_specs=[_token_spec(tm, 1), _resident((1, half))],
        out_specs=[_token_spec(tm, half), _token_spec(tm, half)],
        out_shape=[table, table],
        compiler_params=_compiler_params(),
        name="rope_tables",
    )(positions.astype(F32).reshape(b, s, 1), inv_freq.reshape(1, half))


def _ret_proj_kernel(x_ref, g_ref, win_ref, cos_ref, sin_ref,
                     q_ref, k_ref, v_ref, sg_ref, hb_ref):
    hb_ref[...] = _rmsnorm(x_ref[0], g_ref[...]).astype(BF16)
    half = RET_HEAD_QK // 2
    cos = cos_ref[0]
    sin = sin_ref[0]
    v0 = 2 * RET_QK_DIM
    g0 = v0 + RET_V_DIM
    for h in range(RET_HEADS):
        lo = h * RET_HEAD_V
        v = jnp.dot(hb_ref[...], win_ref[:, v0 + lo:v0 + lo + RET_HEAD_V],
                    preferred_element_type=F32)
        v_ref[0, :, lo:lo + RET_HEAD_V] = v.astype(BF16)
        gt = jnp.dot(hb_ref[...], win_ref[:, g0 + lo:g0 + lo + RET_HEAD_V],
                     preferred_element_type=F32)
        sg_ref[0, :, lo:lo + RET_HEAD_V] = (gt * _sigmoid(gt)).astype(BF16)

    def rotary(col, scale, out_ref, out_col):
        t = jnp.dot(hb_ref[...], win_ref[:, col:col + RET_HEAD_QK], preferred_element_type=F32)
        t1, t2 = t[:, :half], t[:, half:]
        out_ref[0, :, out_col:out_col + half] = ((t1 * cos - t2 * sin) * scale).astype(BF16)
        out_ref[0, :, out_col + half:out_col + RET_HEAD_QK] = (
            (t2 * cos + t1 * sin) * scale).astype(BF16)

    for h in range(RET_HEADS):
        rotary(h * RET_HEAD_QK, 1.0, q_ref, h * RET_HEAD_QK)
        rotary(RET_QK_DIM + h * RET_HEAD_QK, RET_HEAD_QK ** -0.5, k_ref, h * RET_HEAD_QK)


def _ret_proj(x, layer, g, w_in, cos, sin, *, tm=TOKEN_TILE):
    b, s, d = x.shape
    half = RET_HEAD_QK // 2
    width = 2 * RET_QK_DIM + 2 * RET_V_DIM
    return pl.pallas_call(
        _ret_proj_kernel,
        grid=(b, s // tm),
        in_specs=[
            _token_spec(tm, d),
            _layer_resident((1, d), layer),
            _layer_resident((d, width), layer // 2),
            _token_spec(tm, half),
            _token_spec(tm, half),
        ],
        out_specs=[
            _token_spec(tm, RET_QK_DIM),
            _token_spec(tm, RET_QK_DIM),
            _token_spec(tm, RET_V_DIM),
            _token_spec(tm, RET_V_DIM),
        ],
        out_shape=[
            jax.ShapeDtypeStruct((b, s, RET_QK_DIM), BF16),
            jax.ShapeDtypeStruct((b, s, RET_QK_DIM), BF16),
            jax.ShapeDtypeStruct((b, s, RET_V_DIM), BF16),
            jax.ShapeDtypeStruct((b, s, RET_V_DIM), BF16),
        ],
        scratch_shapes=[pltpu.VMEM((tm, d), BF16)],
        compiler_params=_compiler_params(),
        name="ret_proj",
    )(x, g, w_in, cos, sin)


def _ret_scan_kernel(x_ref, q_ref, k_ref, v_ref, sg_ref, intra_ref, qd_ref, kd_ref, cd_ref,
                     gng_ref, wout_ref, o_ref, state_ref, og_ref, *, tm):
    @pl.when(pl.program_id(1) == 0)
    def _():
        state_ref[...] = jnp.zeros_like(state_ref)

    c = RET_CHUNK
    for h in range(RET_HEADS):
        qk = slice(h * RET_HEAD_QK, (h + 1) * RET_HEAD_QK)
        vv = slice(h * RET_HEAD_V, (h + 1) * RET_HEAD_V)
        qd = qd_ref[h]
        kd = kd_ref[h]
        for n in range(tm // c):
            rows = slice(n * c, (n + 1) * c)
            qi = q_ref[0, rows, qk]
            ki = k_ref[0, rows, qk]
            vi = v_ref[0, rows, vv]
            scores = lax.dot_general(qi, ki, (((1,), (1,)), ((), ())),
                                     preferred_element_type=F32) * intra_ref[h]
            inner = jnp.dot(scores.astype(BF16), vi, preferred_element_type=F32)
            state = state_ref[h]
            cross = jnp.dot(qi, state.astype(BF16), preferred_element_type=F32) * qd
            kdec = (ki.astype(F32) * kd).astype(BF16)
            kv = lax.dot_general(kdec, vi, (((0,), (0,)), ((), ())),
                                 preferred_element_type=F32)
            state_ref[h] = cd_ref[h] * state + kv
            on = _center_scale(inner + cross) * gng_ref[:, vv]
            og_ref[rows, vv] = (sg_ref[0, rows, vv].astype(F32) * on).astype(BF16)

    o_ref[0] = x_ref[0] + jnp.dot(og_ref[...], wout_ref[...], preferred_element_type=F32)


def _decay_tables():
    c = RET_CHUNK
    log_gamma = jnp.log1p(-jnp.exp2(-5.0 - jnp.arange(RET_HEADS, dtype=F32)))
    idx = jnp.arange(c, dtype=F32)
    dist = idx[:, None] - idx[None, :]
    intra = jnp.where(dist[None] >= 0,
                      jnp.exp(log_gamma[:, None, None] * jnp.maximum(dist, 0.0)[None]), 0.0)
    q_decay = jnp.exp(log_gamma[:, None] * (idx + 1.0)[None])
    k_decay = jnp.exp(log_gamma[:, None] * (c - 1.0 - idx)[None])
    chunk_decay = jnp.exp(log_gamma * c)
    return (intra,
            jnp.broadcast_to(q_decay[:, :, None], (RET_HEADS, c, RET_HEAD_V)),
            jnp.broadcast_to(k_decay[:, :, None], (RET_HEADS, c, RET_HEAD_QK)),
            jnp.broadcast_to(chunk_decay[:, None, None], (RET_HEADS, 1, RET_HEAD_V)))


def _ret_scan(x, layer, q, k, v, sg, gn_g, w_out, *, tm=TOKEN_TILE):
    b, s, d = x.shape
    intra, qd, kd, cd = _decay_tables()
    return pl.pallas_call(
        functools.partial(_ret_scan_kernel, tm=tm),
        grid=(b, s // tm),
        in_specs=[
            _token_spec(tm, d),
            _token_spec(tm, RET_QK_DIM),
            _token_spec(tm, RET_QK_DIM),
            _token_spec(tm, RET_V_DIM),
            _token_spec(tm, RET_V_DIM),
            _resident(intra.shape),
            _resident(qd.shape),
            _resident(kd.shape),
            _resident(cd.shape),
            _layer_resident((1, RET_V_DIM), layer // 2),
            _layer_resident((RET_V_DIM, d), layer // 2),
        ],
        out_specs=_token_spec(tm, d),
        out_shape=jax.ShapeDtypeStruct(x.shape, x.dtype),
        scratch_shapes=[
            pltpu.VMEM((RET_HEADS, RET_HEAD_QK, RET_HEAD_V), F32),
            pltpu.VMEM((tm, RET_V_DIM), BF16),
        ],
        compiler_params=_compiler_params(),
        name="ret_scan",
    )(x, q, k, v, sg, intra, qd, kd, cd, gn_g, w_out)


def kernel(x, positions, conv_w_in, conv_dw_w, conv_dw_b, conv_ln_g, conv_ln_b, conv_w_out,
           ret_w_in, ret_gn_g, ret_w_out, ffn_w_in, ffn_dw_w, ffn_dw_b, ffn_w_out,
           norm_mix_g, norm_ffn_g, final_g):
    depth = norm_mix_g.shape[0]
    d = x.shape[-1]
    conv_w_in, conv_w_out = conv_w_in.astype(BF16), conv_w_out.astype(BF16)
    ret_w_in, ret_w_out = ret_w_in.astype(BF16), ret_w_out.astype(BF16)
    ffn_w_in, ffn_w_out = ffn_w_in.astype(BF16), ffn_w_out.astype(BF16)
    conv_dw_b, conv_ln_g, conv_ln_b = _rows(conv_dw_b), _rows(conv_ln_g), _rows(conv_ln_b)
    ret_gn_g, ffn_dw_b = _rows(ret_gn_g), _rows(ffn_dw_b)
    norm_mix_g, norm_ffn_g = _rows(norm_mix_g), _rows(norm_ffn_g)
    final_g = final_g.reshape(1, d)
    cos, sin = _rope_tables(positions)

    for i in range(depth):
        final_norm = i == depth - 1
        if i % 2 == 0:
            x = _conv_module_ffn(x, i, norm_mix_g, conv_w_in, conv_dw_w, conv_dw_b,
                                 conv_ln_g, conv_ln_b, conv_w_out,
                                 norm_ffn_g, ffn_w_in, ffn_dw_w, ffn_dw_b, ffn_w_out, final_g,
                                 final_norm=final_norm)
        else:
            q, k, v, sg = _ret_proj(x, i, norm_mix_g, ret_w_in, cos, sin)
            x = _ret_scan(x, i, q, k, v, sg, ret_gn_g, ret_w_out)
            x = _conv_ffn(x, i, norm_ffn_g, ffn_w_in, ffn_dw_w, ffn_dw_b, ffn_w_out, final_g,
                          final_norm=final_norm)
    return x
```

```python
import functools

import jax
import jax.numpy as jnp
from jax import lax
from jax.experimental import pallas as pl
from jax.experimental.pallas import tpu as pltpu

D_MODEL = 1024
CONV_WIDTH = 31
RET_HEADS = 4
RET_HEAD_QK = 256
RET_HEAD_V = 512
RET_QK_DIM = RET_HEADS * RET_HEAD_QK
RET_V_DIM = RET_HEADS * RET_HEAD_V
RET_CHUNK = 256
ROPE_BASE = 10000.0
FFN_DIM = 2816
FFN_CONV_WIDTH = 3
EPS = 1e-6

F32 = jnp.float32
BF16 = jnp.bfloat16

V7X_VMEM_BYTES = 64 * 1024 * 1024
VMEM_LIMIT_BYTES = V7X_VMEM_BYTES - 8 * 1024 * 1024
SUBLANES = 8
LANES = 128

TOKEN_TILE = 512
FFN_COL_TILE = 256
CONV_HIST = 32
CONV_ROW_BLOCK = 64
CONV_COL_TILE = 256


def _resident(shape):
    zeros = (0,) * len(shape)
    return pl.BlockSpec(shape, lambda *_: zeros, pipeline_mode=pl.Buffered(1))


def _layer_resident(shape, layer):
    index = (layer,) + (0,) * len(shape)
    return pl.BlockSpec((None,) + tuple(shape), lambda *_: index, pipeline_mode=pl.Buffered(1))


def _rows(p):
    return p.reshape(p.shape[0], 1, p.shape[1])


def _token_spec(tm, width):
    return pl.BlockSpec((1, tm, width), lambda b, i: (b, i, 0))


def _compiler_params():
    return pltpu.CompilerParams(
        dimension_semantics=("arbitrary", "arbitrary"),
        vmem_limit_bytes=VMEM_LIMIT_BYTES,
    )


def _rmsnorm(x, g):
    ms = jnp.mean(x * x, axis=-1, keepdims=True)
    return x * lax.rsqrt(ms + EPS) * g


def _center_scale(x):
    mu = jnp.mean(x, axis=-1, keepdims=True)
    xc = x - mu
    var = jnp.mean(xc * xc, axis=-1, keepdims=True)
    return xc * lax.rsqrt(var + EPS)


def _sigmoid(x):
    return 1.0 / (1.0 + jnp.exp(-x))


def _shift_rows(a, prev, k):
    rolled = pltpu.roll(a, k, 0)
    top = rolled[:SUBLANES]
    row = lax.broadcasted_iota(jnp.int32, top.shape, 0)
    for r in range(k):
        top = jnp.where(row == r, prev[SUBLANES - k + r:SUBLANES - k + r + 1, :], top)
    return jnp.concatenate([top, rolled[SUBLANES:]], axis=0)


def _ffn_hidden(hb_ref, win_ref, dww_ref, dwb_ref, act_ref, carry_ref, *, tm, fc,
                after_chunk=None):
    for c in range(FFN_DIM // fc):
        lo = c * fc
        hb = hb_ref[...]
        a = jnp.dot(hb, win_ref[:, lo:lo + fc], preferred_element_type=F32)
        u = jnp.dot(hb, win_ref[:, FFN_DIM + lo:FFN_DIM + lo + fc],
                    preferred_element_type=F32)
        prev = carry_ref[:, lo:lo + fc]
        carry_ref[:, lo:lo + fc] = a[tm - SUBLANES:, :]
        a1 = _shift_rows(a, prev, 1)
        a2 = _shift_rows(a, prev, 2)
        conv = (dww_ref[2:3, lo:lo + fc] * a + dww_ref[1:2, lo:lo + fc] * a1
                + dww_ref[0:1, lo:lo + fc] * a2 + dwb_ref[:, lo:lo + fc])
        act_ref[:, lo:lo + fc] = (conv * _sigmoid(conv) * u).astype(BF16)
        if after_chunk is not None:
            after_chunk(c)


def _ffn_kernel(x_ref, g_ref, win_ref, dww_ref, dwb_ref, wout_ref, fg_ref, o_ref,
                hb_ref, act_ref, carry_ref, *, tm, fc, final_norm):
    x = x_ref[0]
    hb_ref[...] = _rmsnorm(x, g_ref[...]).astype(BF16)

    @pl.when(pl.program_id(1) == 0)
    def _():
        carry_ref[...] = jnp.zeros_like(carry_ref)

    _ffn_hidden(hb_ref, win_ref, dww_ref, dwb_ref, act_ref, carry_ref, tm=tm, fc=fc)
    y = x + jnp.dot(act_ref[...], wout_ref[...], preferred_element_type=F32)
    o_ref[0] = _rmsnorm(y, fg_ref[...]) if final_norm else y


def _conv_ffn(x, layer, g, w_in, dw_w, dw_b, w_out, final_g, *, final_norm,
              tm=TOKEN_TILE, fc=FFN_COL_TILE):
    b, s, d = x.shape
    f = FFN_DIM
    return pl.pallas_call(
        functools.partial(_ffn_kernel, tm=tm, fc=fc, final_norm=final_norm),
        grid=(b, s // tm),
        in_specs=[
            _token_spec(tm, d),
            _layer_resident((1, d), layer),
            _layer_resident((d, 2 * f), layer),
            _layer_resident((FFN_CONV_WIDTH, f), layer),
            _layer_resident((1, f), layer),
            _layer_resident((f, d), layer),
            _resident((1, d)),
        ],
        out_specs=_token_spec(tm, d),
        out_shape=jax.ShapeDtypeStruct(x.shape, x.dtype),
        scratch_shapes=[
            pltpu.VMEM((tm, d), BF16),
            pltpu.VMEM((tm, f), BF16),
            pltpu.VMEM((SUBLANES, f), F32),
        ],
        compiler_params=_compiler_params(),
        name="conv_ffn",
    )(x, g, w_in, dw_w, dw_b, w_out, final_g)


def _conv_module_jobs(win_ref, dww_ref, dwb_ref, hb_ref, buf_ref, y_ref, *, tm):
    d = D_MODEL
    base = CONV_HIST - (CONV_WIDTH - 1)
    rb = CONV_ROW_BLOCK
    cw = CONV_COL_TILE

    def glu(cb):
        hb = hb_ref[...]
        a = jnp.dot(hb, win_ref[:, cb:cb + cw], preferred_element_type=F32)
        gate = jnp.dot(hb, win_ref[:, d + cb:d + cb + cw], preferred_element_type=F32)
        buf_ref[CONV_HIST:CONV_HIST + tm, cb:cb + cw] = a * _sigmoid(gate)

    def conv_block(r0, c0):
        total = None
        for phase in range(SUBLANES):
            part = None
            for j in range(CONV_WIDTH):
                off = base + j
                if off % SUBLANES != phase:
                    continue
                term = (dww_ref[j:j + 1, c0:c0 + LANES]
                        * buf_ref[r0 + off:r0 + off + rb, c0:c0 + LANES])
                part = term if part is None else part + term
            if part is not None:
                total = part if total is None else total + part
        y_ref[r0:r0 + rb, c0:c0 + LANES] = total + dwb_ref[:, c0:c0 + LANES]

    jobs = []
    for cb in range(0, d, cw):
        jobs.append(functools.partial(glu, cb))
        for c0 in range(cb, cb + cw, LANES):
            for r0 in range(0, tm, rb):
                jobs.append(functools.partial(conv_block, r0, c0))
    return jobs


def _conv_ffn_fused_kernel(x_ref, gm_ref, cwin_ref, cdww_ref, cdwb_ref, lng_ref, lnb_ref,
                           cwout_ref, gf_ref, fwin_ref, fdww_ref, fdwb_ref, fwout_ref, fg_ref,
                           o_ref, x1_ref, hbc_ref, buf_ref, y_ref, hb_ref, act_ref, carry_ref,
                           *, tm, fc, nt, final_norm):
    d = D_MODEL
    n = pl.program_id(0)
    conv_seq_start = n % nt == 0
    ffn_seq_start = (n - 1) % nt == 0

    @pl.when(n == 0)
    def _():
        x1_ref[...] = jnp.zeros_like(x1_ref)

    @pl.when(conv_seq_start)
    def _():
        buf_ref[0:CONV_HIST, :] = jnp.zeros((CONV_HIST, d), F32)

    @pl.when(jnp.logical_not(conv_seq_start))
    def _():
        buf_ref[0:CONV_HIST, :] = buf_ref[tm:tm + CONV_HIST, :]

    @pl.when(jnp.logical_or(ffn_seq_start, n == 0))
    def _():
        carry_ref[...] = jnp.zeros_like(carry_ref)

    x1 = x1_ref[...]
    hb_ref[...] = _rmsnorm(x1, gf_ref[...]).astype(BF16)
    x = x_ref[0]
    hbc_ref[...] = _rmsnorm(x, gm_ref[...]).astype(BF16)

    jobs = _conv_module_jobs(cwin_ref, cdww_ref, cdwb_ref, hbc_ref, buf_ref, y_ref, tm=tm)
    per_chunk = -(-len(jobs) // (FFN_DIM // fc))

    def conv_jobs_after_chunk(c):
        for job in jobs[c * per_chunk:(c + 1) * per_chunk]:
            job()

    _ffn_hidden(hb_ref, fwin_ref, fdww_ref, fdwb_ref, act_ref, carry_ref, tm=tm, fc=fc,
                after_chunk=conv_jobs_after_chunk)
    y = x1 + jnp.dot(act_ref[...], fwout_ref[...], preferred_element_type=F32)
    o_ref[0] = _rmsnorm(y, fg_ref[...]) if final_norm else y

    yn = _center_scale(y_ref[...]) * lng_ref[...] + lnb_ref[...]
    act = (yn * _sigmoid(yn)).astype(BF16)
    x1_ref[...] = x + jnp.dot(act, cwout_ref[...], preferred_element_type=F32)


def _conv_module_ffn(x, layer, mix_g, c_w_in, c_dw_w, c_dw_b, ln_g, ln_b, c_w_out,
                     ffn_g, f_w_in, f_dw_w, f_dw_b, f_w_out, final_g, *, final_norm,
                     tm=TOKEN_TILE, fc=FFN_COL_TILE):
    b, s, d = x.shape
    f = FFN_DIM
    nt = s // tm
    ntiles = b * nt

    def conv_tile(n):
        m = jnp.minimum(n, ntiles - 1)
        return (m // nt, m % nt, 0)

    def ffn_tile(n):
        m = jnp.maximum(n - 1, 0)
        return (m // nt, m % nt, 0)

    return pl.pallas_call(
        functools.partial(_conv_ffn_fused_kernel, tm=tm, fc=fc, nt=nt, final_norm=final_norm),
        grid=(ntiles + 1,),
        in_specs=[
            pl.BlockSpec((1, tm, d), conv_tile),
            _layer_resident((1, d), layer),
            _layer_resident((d, 2 * d), layer // 2),
            _layer_resident((CONV_WIDTH, d), layer // 2),
            _layer_resident((1, d), layer // 2),
            _layer_resident((1, d), layer // 2),
            _layer_resident((1, d), layer // 2),
            _layer_resident((d, d), layer // 2),
            _layer_resident((1, d), layer),
            _layer_resident((d, 2 * f), layer),
            _layer_resident((FFN_CONV_WIDTH, f), layer),
            _layer_resident((1, f), layer),
            _layer_resident((f, d), layer),
            _resident((1, d)),
        ],
        out_specs=pl.BlockSpec((1, tm, d), ffn_tile),
        out_shape=jax.ShapeDtypeStruct(x.shape, x.dtype),
        scratch_shapes=[
            pltpu.VMEM((tm, d), F32),
            pltpu.VMEM((tm, d), BF16),
            pltpu.VMEM((CONV_HIST + tm, d), F32),
            pltpu.VMEM((tm, d), F32),
            pltpu.VMEM((tm, d), BF16),
            pltpu.VMEM((tm, f), BF16),
            pltpu.VMEM((SUBLANES, f), F32),
        ],
        compiler_params=pltpu.CompilerParams(
            dimension_semantics=("arbitrary",),
            vmem_limit_bytes=VMEM_LIMIT_BYTES,
        ),
        name="conv_module_ffn",
    )(x, mix_g, c_w_in, c_dw_w, c_dw_b, ln_g, ln_b, c_w_out,
      ffn_g, f_w_in, f_dw_w, f_dw_b, f_w_out, final_g)


def _rope_kernel(pos_ref, invf_ref, cos_ref, sin_ref):
    ang = pos_ref[0] * invf_ref[...]
    cos_ref[0] = jnp.cos(ang)
    sin_ref[0] = jnp.sin(ang)


def _rope_tables(positions, *, tm=TOKEN_TILE):
    b, s = positions.shape
    half = RET_HEAD_QK // 2
    inv_freq = 1.0 / (ROPE_BASE ** (jnp.arange(0, RET_HEAD_QK, 2, dtype=F32) / RET_HEAD_QK))
    table = jax.ShapeDtypeStruct((b, s, half), F32)
    return pl.pallas_call(
        _rope_kernel,
        grid=(b, s // tm),
        in_specs=[_token_spec(tm, 1), _resident((1, half))],
        out_specs=[_token_spec(tm, half), _token_spec(tm, half)],
        out_shape=[table, table],
        compiler_params=_compiler_params(),
        name="rope_tables",
    )(positions.astype(F32).reshape(b, s, 1), inv_freq.reshape(1, half))


def _ret_proj_kernel(x_ref, g_ref, win_ref, cos_ref, sin_ref,
                     q_ref, k_ref, v_ref, sg_ref, hb_ref):
    hb_ref[...] = _rmsnorm(x_ref[0], g_ref[...]).astype(BF16)
    half = RET_HEAD_QK // 2
    cos = cos_ref[0]
    sin = sin_ref[0]
    v0 = 2 * RET_QK_DIM
    g0 = v0 + RET_V_DIM
    for h in range(RET_HEADS):
        lo = h * RET_HEAD_V
        v = jnp.dot(hb_ref[...], win_ref[:, v0 + lo:v0 + lo + RET_HEAD_V],
                    preferred_element_type=F32)
        v_ref[0, :, lo:lo + RET_HEAD_V] = v.astype(BF16)
        gt = jnp.dot(hb_ref[...], win_ref[:, g0 + lo:g0 + lo + RET_HEAD_V],
                     preferred_element_type=F32)
        sg_ref[0, :, lo:lo + RET_HEAD_V] = (gt * _sigmoid(gt)).astype(BF16)

    def rotary(col, scale, out_ref, out_col):
        t = jnp.dot(hb_ref[...], win_ref[:, col:col + RET_HEAD_QK], preferred_element_type=F32)
        t1, t2 = t[:, :half], t[:, half:]
        out_ref[0, :, out_col:out_col + half] = ((t1 * cos - t2 * sin) * scale).astype(BF16)
        out_ref[0, :, out_col + half:out_col + RET_HEAD_QK] = (
            (t2 * cos + t1 * sin) * scale).astype(BF16)

    for h in range(RET_HEADS):
        rotary(h * RET_HEAD_QK, 1.0, q_ref, h * RET_HEAD_QK)
        rotary(RET_QK_DIM + h * RET_HEAD_QK, RET_HEAD_QK ** -0.5, k_ref, h * RET_HEAD_QK)


def _ret_proj(x, layer, g, w_in, cos, sin, *, tm=TOKEN_TILE):
    b, s, d = x.shape
    half = RET_HEAD_QK // 2
    width = 2 * RET_QK_DIM + 2 * RET_V_DIM
    return pl.pallas_call(
        _ret_proj_kernel,
        grid=(b, s // tm),
        in_specs=[
            _token_spec(tm, d),
            _layer_resident((1, d), layer),
            _layer_resident((d, width), layer // 2),
            _token_spec(tm, half),
            _token_spec(tm, half),
        ],
        out_specs=[
            _token_spec(tm, RET_QK_DIM),
            _token_spec(tm, RET_QK_DIM),
            _token_spec(tm, RET_V_DIM),
            _token_spec(tm, RET_V_DIM),
        ],
        out_shape=[
            jax.ShapeDtypeStruct((b, s, RET_QK_DIM), BF16),
            jax.ShapeDtypeStruct((b, s, RET_QK_DIM), BF16),
            jax.ShapeDtypeStruct((b, s, RET_V_DIM), BF16),
            jax.ShapeDtypeStruct((b, s, RET_V_DIM), BF16),
        ],
        scratch_shapes=[pltpu.VMEM((tm, d), BF16)],
        compiler_params=_compiler_params(),
        name="ret_proj",
    )(x, g, w_in, cos, sin)


def _ret_scan_kernel(x_ref, q_ref, k_ref, v_ref, sg_ref, intra_ref, qd_ref, kd_ref, cd_ref,
                     gng_ref, wout_ref, o_ref, state_ref, og_ref, *, tm):
    @pl.when(pl.program_id(1) == 0)
    def _():
        state_ref[...] = jnp.zeros_like(state_ref)

    c = RET_CHUNK
    for h in range(RET_HEADS):
        qk = slice(h * RET_HEAD_QK, (h + 1) * RET_HEAD_QK)
        vv = slice(h * RET_HEAD_V, (h + 1) * RET_HEAD_V)
        qd = qd_ref[h]
        kd = kd_ref[h]
        for n in range(tm // c):
            rows = slice(n * c, (n + 1) * c)
            qi = q_ref[0, rows, qk]
            ki = k_ref[0, rows, qk]
            vi = v_ref[0, rows, vv]
            scores = lax.dot_general(qi, ki, (((1,), (1,)), ((), ())),
                                     preferred_element_type=F32) * intra_ref[h]
            inner = jnp.dot(scores.astype(BF16), vi, preferred_element_type=F32)
            state = state_ref[h]
            cross = jnp.dot(qi, state.astype(BF16), preferred_element_type=F32) * qd
            kdec = (ki.astype(F32) * kd).astype(BF16)
            kv = lax.dot_general(kdec, vi, (((0,), (0,)), ((), ())),
                                 preferred_element_type=F32)
            state_ref[h] = cd_ref[h] * state + kv
            on = _center_scale(inner + cross) * gng_ref[:, vv]
            og_ref[rows, vv] = (sg_ref[0, rows, vv].astype(F32) * on).astype(BF16)

    o_ref[0] = x_ref[0] + jnp.dot(og_ref[...], wout_ref[...], preferred_element_type=F32)


def _decay_tables():
    c = RET_CHUNK
    log_gamma = jnp.log1p(-jnp.exp2(-5.0 - jnp.arange(RET_HEADS, dtype=F32)))
    idx = jnp.arange(c, dtype=F32)
    dist = idx[:, None] - idx[None, :]
    intra = jnp.where(dist[None] >= 0,
                      jnp.exp(log_gamma[:, None, None] * jnp.maximum(dist, 0.0)[None]), 0.0)
    q_decay = jnp.exp(log_gamma[:, None] * (idx + 1.0)[None])
    k_decay = jnp.exp(log_gamma[:, None] * (c - 1.0 - idx)[None])
    chunk_decay = jnp.exp(log_gamma * c)
    return (intra,
            jnp.broadcast_to(q_decay[:, :, None], (RET_HEADS, c, RET_HEAD_V)),
            jnp.broadcast_to(k_decay[:, :, None], (RET_HEADS, c, RET_HEAD_QK)),
            jnp.broadcast_to(chunk_decay[:, None, None], (RET_HEADS, 1, RET_HEAD_V)))


def _ret_scan(x, layer, q, k, v, sg, gn_g, w_out, *, tm=TOKEN_TILE):
    b, s, d = x.shape
    intra, qd, kd, cd = _decay_tables()
    return pl.pallas_call(
        functools.partial(_ret_scan_kernel, tm=tm),
        grid=(b, s // tm),
        in_specs=[
            _token_spec(tm, d),
            _token_spec(tm, RET_QK_DIM),
            _token_spec(tm, RET_QK_DIM),
            _token_spec(tm, RET_V_DIM),
            _token_spec(tm, RET_V_DIM),
            _resident(intra.shape),
            _resident(qd.shape),
            _resident(kd.shape),
            _resident(cd.shape),
            _layer_resident((1, RET_V_DIM), layer // 2),
            _layer_resident((RET_V_DIM, d), layer // 2),
        ],
        out_specs=_token_spec(tm, d),
        out_shape=jax.ShapeDtypeStruct(x.shape, x.dtype),
        scratch_shapes=[
            pltpu.VMEM((RET_HEADS, RET_HEAD_QK, RET_HEAD_V), F32),
            pltpu.VMEM((tm, RET_V_DIM), BF16),
        ],
        compiler_params=_compiler_params(),
        name="ret_scan",
    )(x, q, k, v, sg, intra, qd, kd, cd, gn_g, w_out)


def kernel(x, positions, conv_w_in, conv_dw_w, conv_dw_b, conv_ln_g, conv_ln_b, conv_w_out,
           ret_w_in, ret_gn_g, ret_w_out, ffn_w_in, ffn_dw_w, ffn_dw_b, ffn_w_out,
           norm_mix_g, norm_ffn_g, final_g):
    depth = norm_mix_g.shape[0]
    d = x.shape[-1]
    conv_w_in, conv_w_out = conv_w_in.astype(BF16), conv_w_out.astype(BF16)
    ret_w_in, ret_w_out = ret_w_in.astype(BF16), ret_w_out.astype(BF16)
    ffn_w_in, ffn_w_out = ffn_w_in.astype(BF16), ffn_w_out.astype(BF16)
    conv_dw_b, conv_ln_g, conv_ln_b = _rows(conv_dw_b), _rows(conv_ln_g), _rows(conv_ln_b)
    ret_gn_g, ffn_dw_b = _rows(ret_gn_g), _rows(ffn_dw_b)
    norm_mix_g, norm_ffn_g = _rows(norm_mix_g), _rows(norm_ffn_g)
    final_g = final_g.reshape(1, d)
    cos, sin = _rope_tables(positions)

    for i in range(depth):
        final_norm = i == depth - 1
        if i % 2 == 0:
            x = _conv_module_ffn(x, i, norm_mix_g, conv_w_in, conv_dw_w, conv_dw_b,
                                 conv_ln_g, conv_ln_b, conv_w_out,
                                 norm_ffn_g, ffn_w_in, ffn_dw_w, ffn_dw_b, ffn_w_out, final_g,
                                 final_norm=final_norm)
        else:
            q, k, v, sg = _ret_proj(x, i, norm_mix_g, ret_w_in, cos, sin)
            x = _ret_scan(x, i, q, k, v, sg, ret_gn_g, ret_w_out)
            x = _conv_ffn(x, i, norm_ffn_g, ffn_w_in, ffn_dw_w, ffn_dw_b, ffn_w_out, final_g,
                          final_norm=final_norm)
    return x
```

```python
import functools

import jax
import jax.numpy as jnp
from jax import lax
from jax.experimental import pallas as pl
from jax.experimental.pallas import tpu as pltpu

D_MODEL = 1024
CONV_WIDTH = 31
RET_HEADS = 4
RET_HEAD_QK = 256
RET_HEAD_V = 512
RET_QK_DIM = RET_HEADS * RET_HEAD_QK
RET_V_DIM = RET_HEADS * RET_HEAD_V
RET_CHUNK = 256
ROPE_BASE = 10000.0
FFN_DIM = 2816
FFN_CONV_WIDTH = 3
EPS = 1e-6

F32 = jnp.float32
BF16 = jnp.bfloat16

V7X_VMEM_BYTES = 64 * 1024 * 1024
VMEM_LIMIT_BYTES = V7X_VMEM_BYTES - 8 * 1024 * 1024
SUBLANES = 8
LANES = 128
BF16_ROW_TILE = 16

TOKEN_TILE = 512
FFN_COL_TILE = 256
CONV_HIST = 32
CONV_ROW_BLOCK = 64
CONV_COL_TILE = 256

WEIGHT_LOAD_STEPS = 16


def _resident(shape):
    zeros = (0,) * len(shape)
    return pl.BlockSpec(shape, lambda n: zeros, pipeline_mode=pl.Buffered(1))


def _layer_resident(shape, layer):
    index = (layer,) + (0,) * len(shape)
    return pl.BlockSpec((None,) + tuple(shape), lambda n: index, pipeline_mode=pl.Buffered(1))


def _weight_chunk_spec(shape, layer):
    rows, cols = shape
    assert rows % (WEIGHT_LOAD_STEPS * BF16_ROW_TILE) == 0, shape
    return pl.BlockSpec((None, rows // WEIGHT_LOAD_STEPS, cols),
                        lambda n: (layer, jnp.minimum(n, WEIGHT_LOAD_STEPS - 1), 0))


def _stage_weights(n, pairs):
    @pl.when(n < WEIGHT_LOAD_STEPS)
    def _():
        for chunk_ref, dst_ref in pairs:
            rows = chunk_ref.shape[0]
            start = pl.multiple_of(n * rows, rows)
            dst_ref[pl.ds(start, rows), :] = chunk_ref[...].astype(BF16)


def _tile_spec(tm, width, nt, ntiles, lag=0):
    def index(n):
        m = jnp.clip(n - WEIGHT_LOAD_STEPS - lag, 0, ntiles - 1)
        return (m // nt, m % nt, 0)
    return pl.BlockSpec((1, tm, width), index)


def _rows(p):
    return p.reshape(p.shape[0], 1, p.shape[1])


def _compiler_params():
    return pltpu.CompilerParams(
        dimension_semantics=("arbitrary",),
        vmem_limit_bytes=VMEM_LIMIT_BYTES,
    )


def _rmsnorm(x, g):
    ms = jnp.mean(x * x, axis=-1, keepdims=True)
    return x * lax.rsqrt(ms + EPS) * g


def _center_scale(x):
    mu = jnp.mean(x, axis=-1, keepdims=True)
    xc = x - mu
    var = jnp.mean(xc * xc, axis=-1, keepdims=True)
    return xc * lax.rsqrt(var + EPS)


def _sigmoid(x):
    return 1.0 / (1.0 + jnp.exp(-x))


def _shift_rows(a, prev, k):
    rolled = pltpu.roll(a, k, 0)
    top = rolled[:SUBLANES]
    row = lax.broadcasted_iota(jnp.int32, top.shape, 0)
    for r in range(k):
        top = jnp.where(row == r, prev[SUBLANES - k + r:SUBLANES - k + r + 1, :], top)
    return jnp.concatenate([top, rolled[SUBLANES:]], axis=0)


def _ffn_hidden(hb_ref, win_ref, dww_ref, dwb_ref, act_ref, carry_ref, *, tm, fc,
                after_chunk=None):
    for c in range(FFN_DIM // fc):
        lo = c * fc
        hb = hb_ref[...]
        a = jnp.dot(hb, win_ref[:, lo:lo + fc], preferred_element_type=F32)
        u = jnp.dot(hb, win_ref[:, FFN_DIM + lo:FFN_DIM + lo + fc],
                    preferred_element_type=F32)
        prev = carry_ref[:, lo:lo + fc]
        carry_ref[:, lo:lo + fc] = a[tm - SUBLANES:, :]
        a1 = _shift_rows(a, prev, 1)
        a2 = _shift_rows(a, prev, 2)
        conv = (dww_ref[2:3, lo:lo + fc] * a + dww_ref[1:2, lo:lo + fc] * a1
                + dww_ref[0:1, lo:lo + fc] * a2 + dwb_ref[:, lo:lo + fc])
        act_ref[:, lo:lo + fc] = (conv * _sigmoid(conv) * u).astype(BF16)
        if after_chunk is not None:
            after_chunk(c)


def _ffn_kernel(x_ref, g_ref, win32_ref, dww_ref, dwb_ref, wout32_ref, fg_ref, o_ref,
                win_ref, wout_ref, hb_ref, act_ref, carry_ref, *, tm, fc, nt, final_norm):
    n = pl.program_id(0)
    _stage_weights(n, [(win32_ref, win_ref), (wout32_ref, wout_ref)])
    tile = n - WEIGHT_LOAD_STEPS

    @pl.when(tile >= 0)
    def _():
        @pl.when(tile % nt == 0)
        def _():
            carry_ref[...] = jnp.zeros_like(carry_ref)

        x = x_ref[0]
        hb_ref[...] = _rmsnorm(x, g_ref[...]).astype(BF16)
        _ffn_hidden(hb_ref, win_ref, dww_ref, dwb_ref, act_ref, carry_ref, tm=tm, fc=fc)
        y = x + jnp.dot(act_ref[...], wout_ref[...], preferred_element_type=F32)
        o_ref[0] = _rmsnorm(y, fg_ref[...]) if final_norm else y


def _conv_ffn(x, layer, g, w_in, dw_w, dw_b, w_out, final_g, *, final_norm,
              tm=TOKEN_TILE, fc=FFN_COL_TILE):
    b, s, d = x.shape
    f = FFN_DIM
    nt = s // tm
    ntiles = b * nt
    return pl.pallas_call(
        functools.partial(_ffn_kernel, tm=tm, fc=fc, nt=nt, final_norm=final_norm),
        grid=(WEIGHT_LOAD_STEPS + ntiles,),
        in_specs=[
            _tile_spec(tm, d, nt, ntiles),
            _layer_resident((1, d), layer),
            _weight_chunk_spec((d, 2 * f), layer),
            _layer_resident((FFN_CONV_WIDTH, f), layer),
            _layer_resident((1, f), layer),
            _weight_chunk_spec((f, d), layer),
            _resident((1, d)),
        ],
        out_specs=_tile_spec(tm, d, nt, ntiles),
        out_shape=jax.ShapeDtypeStruct(x.shape, x.dtype),
        scratch_shapes=[
            pltpu.VMEM((d, 2 * f), BF16),
            pltpu.VMEM((f, d), BF16),
            pltpu.VMEM((tm, d), BF16),
            pltpu.VMEM((tm, f), BF16),
            pltpu.VMEM((SUBLANES, f), F32),
        ],
        compiler_params=_compiler_params(),
        name="conv_ffn",
    )(x, g, w_in, dw_w, dw_b, w_out, final_g)


def _conv_module_jobs(win_ref, dww_ref, dwb_ref, hb_ref, buf_ref, y_ref, *, tm):
    d = D_MODEL
    base = CONV_HIST - (CONV_WIDTH - 1)
    rb = CONV_ROW_BLOCK
    cw = CONV_COL_TILE

    def glu(cb):
        hb = hb_ref[...]
        a = jnp.dot(hb, win_ref[:, cb:cb + cw], preferred_element_type=F32)
        gate = jnp.dot(hb, win_ref[:, d + cb:d + cb + cw], preferred_element_type=F32)
        buf_ref[CONV_HIST:CONV_HIST + tm, cb:cb + cw] = a * _sigmoid(gate)

    def conv_block(r0, c0):
        total = None
        for phase in range(SUBLANES):
            part = None
            for j in range(CONV_WIDTH):
                off = base + j
                if off % SUBLANES != phase:
                    continue
                term = (dww_ref[j:j + 1, c0:c0 + LANES]
                        * buf_ref[r0 + off:r0 + off + rb, c0:c0 + LANES])
                part = term if part is None else part + term
            if part is not None:
                total = part if total is None else total + part
        y_ref[r0:r0 + rb, c0:c0 + LANES] = total + dwb_ref[:, c0:c0 + LANES]

    jobs = []
    for cb in range(0, d, cw):
        jobs.append(functools.partial(glu, cb))
        for c0 in range(cb, cb + cw, LANES):
            for r0 in range(0, tm, rb):
                jobs.append(functools.partial(conv_block, r0, c0))
    return jobs


def _conv_ffn_fused_kernel(x_ref, gm_ref, cwin32_ref, cdww_ref, cdwb_ref, lng_ref, lnb_ref,
                           cwout32_ref, gf_ref, fwin32_ref, fdww_ref, fdwb_ref, fwout32_ref,
                           fg_ref, o_ref,
                           cwin_ref, cwout_ref, fwin_ref, fwout_ref,
                           x1_ref, hbc_ref, buf_ref, y_ref, hb_ref, act_ref, carry_ref,
                           *, tm, fc, nt, final_norm):
    d = D_MODEL
    n = pl.program_id(0)
    _stage_weights(n, [(cwin32_ref, cwin_ref), (cwout32_ref, cwout_ref),
                       (fwin32_ref, fwin_ref), (fwout32_ref, fwout_ref)])
    k = n - WEIGHT_LOAD_STEPS

    @pl.when(k >= 0)
    def _():
        conv_seq_start = k % nt == 0
        ffn_seq_start = (k - 1) % nt == 0

        @pl.when(k == 0)
        def _():
            x1_ref[...] = jnp.zeros_like(x1_ref)

        @pl.when(conv_seq_start)
        def _():
            buf_ref[0:CONV_HIST, :] = jnp.zeros((CONV_HIST, d), F32)

        @pl.when(jnp.logical_not(conv_seq_start))
        def _():
            buf_ref[0:CONV_HIST, :] = buf_ref[tm:tm + CONV_HIST, :]

        @pl.when(jnp.logical_or(ffn_seq_start, k == 0))
        def _():
            carry_ref[...] = jnp.zeros_like(carry_ref)

        x1 = x1_ref[...]
        hb_ref[...] = _rmsnorm(x1, gf_ref[...]).astype(BF16)
        x = x_ref[0]
        hbc_ref[...] = _rmsnorm(x, gm_ref[...]).astype(BF16)

        jobs = _conv_module_jobs(cwin_ref, cdww_ref, cdwb_ref, hbc_ref, buf_ref, y_ref, tm=tm)
        per_chunk = -(-len(jobs) // (FFN_DIM // fc))

        def conv_jobs_after_chunk(c):
            for job in jobs[c * per_chunk:(c + 1) * per_chunk]:
                job()

        _ffn_hidden(hb_ref, fwin_ref, fdww_ref, fdwb_ref, act_ref, carry_ref, tm=tm, fc=fc,
                    after_chunk=conv_jobs_after_chunk)
        y = x1 + jnp.dot(act_ref[...], fwout_ref[...], preferred_element_type=F32)
        o_ref[0] = _rmsnorm(y, fg_ref[...]) if final_norm else y

        yn = _center_scale(y_ref[...]) * lng_ref[...] + lnb_ref[...]
        act = (yn * _sigmoid(yn)).astype(BF16)
        x1_ref[...] = x + jnp.dot(act, cwout_ref[...], preferred_element_type=F32)


def _conv_module_ffn(x, layer, mix_g, c_w_in, c_dw_w, c_dw_b, ln_g, ln_b, c_w_out,
                     ffn_g, f_w_in, f_dw_w, f_dw_b, f_w_out, final_g, *, final_norm,
                     tm=TOKEN_TILE, fc=FFN_COL_TILE):
    b, s, d = x.shape
    f = FFN_DIM
    nt = s // tm
    ntiles = b * nt
    return pl.pallas_call(
        functools.partial(_conv_ffn_fused_kernel, tm=tm, fc=fc, nt=nt, final_norm=final_norm),
        grid=(WEIGHT_LOAD_STEPS + ntiles + 1,),
        in_specs=[
            _tile_spec(tm, d, nt, ntiles),
            _layer_resident((1, d), layer),
            _weight_chunk_spec((d, 2 * d), layer // 2),
            _layer_resident((CONV_WIDTH, d), layer // 2),
            _layer_resident((1, d), layer // 2),
            _layer_resident((1, d), layer // 2),
            _layer_resident((1, d), layer // 2),
            _weight_chunk_spec((d, d), layer // 2),
            _layer_resident((1, d), layer),
            _weight_chunk_spec((d, 2 * f), layer),
            _layer_resident((FFN_CONV_WIDTH, f), layer),
            _layer_resident((1, f), layer),
            _weight_chunk_spec((f, d), layer),
            _resident((1, d)),
        ],
        out_specs=_tile_spec(tm, d, nt, ntiles, lag=1),
        out_shape=jax.ShapeDtypeStruct(x.shape, x.dtype),
        scratch_shapes=[
            pltpu.VMEM((d, 2 * d), BF16),
            pltpu.VMEM((d, d), BF16),
            pltpu.VMEM((d, 2 * f), BF16),
            pltpu.VMEM((f, d), BF16),
            pltpu.VMEM((tm, d), F32),
            pltpu.VMEM((tm, d), BF16),
            pltpu.VMEM((CONV_HIST + tm, d), F32),
            pltpu.VMEM((tm, d), F32),
            pltpu.VMEM((tm, d), BF16),
            pltpu.VMEM((tm, f), BF16),
            pltpu.VMEM((SUBLANES, f), F32),
        ],
        compiler_params=_compiler_params(),
        name="conv_module_ffn",
    )(x, mix_g, c_w_in, c_dw_w, c_dw_b, ln_g, ln_b, c_w_out,
      ffn_g, f_w_in, f_dw_w, f_dw_b, f_w_out, final_g)


def _rope_kernel(pos_ref, invf_ref, cos_ref, sin_ref):
    ang = pos_ref[0] * invf_ref[...]
    cos_ref[0] = jnp.cos(ang)
    sin_ref[0] = jnp.sin(ang)


def _rope_tables(positions, *, tm=TOKEN_TILE):
    b, s = positions.shape
    half = RET_HEAD_QK // 2
    inv_freq = 1.0 / (ROPE_BASE ** (jnp.arange(0, RET_HEAD_QK, 2, dtype=F32) / RET_HEAD_QK))

    def tile(width):
        return pl.BlockSpec((1, tm, width), lambda bi, i: (bi, i, 0))

    table = jax.ShapeDtypeStruct((b, s, half), F32)
    return pl.pallas_call(
        _rope_kernel,
        grid=(b, s // tm),
        in_specs=[tile(1), pl.BlockSpec((1, half), lambda bi, i: (0, 0))],
        out_specs=[tile(half), tile(half)],
        out_shape=[table, table],
        compiler_params=pltpu.CompilerParams(
            dimension_semantics=("arbitrary", "arbitrary"),
            vmem_limit_bytes=VMEM_LIMIT_BYTES,
        ),
        name="rope_tables",
    )(positions.astype(F32).reshape(b, s, 1), inv_freq.reshape(1, half))


def _ret_proj_kernel(x_ref, g_ref, win32_ref, cos_ref, sin_ref,
                     q_ref, k_ref, v_ref, sg_ref, win_ref, hb_ref):
    n = pl.program_id(0)
    _stage_weights(n, [(win32_ref, win_ref)])

    @pl.when(n >= WEIGHT_LOAD_STEPS)
    def _():
        hb_ref[...] = _rmsnorm(x_ref[0], g_ref[...]).astype(BF16)
        half = RET_HEAD_QK // 2
        cos = cos_ref[0]
        sin = sin_ref[0]
        v0 = 2 * RET_QK_DIM
        g0 = v0 + RET_V_DIM
        for h in range(RET_HEADS):
            lo = h * RET_HEAD_V
            v = jnp.dot(hb_ref[...], win_ref[:, v0 + lo:v0 + lo + RET_HEAD_V],
                        preferred_element_type=F32)
            v_ref[0, :, lo:lo + RET_HEAD_V] = v.astype(BF16)
            gt = jnp.dot(hb_ref[...], win_ref[:, g0 + lo:g0 + lo + RET_HEAD_V],
                         preferred_element_type=F32)
            sg_ref[0, :, lo:lo + RET_HEAD_V] = (gt * _sigmoid(gt)).astype(BF16)

        def rotary(col, scale, out_ref, out_col):
            t = jnp.dot(hb_ref[...], win_ref[:, col:col + RET_HEAD_QK],
                        preferred_element_type=F32)
            t1, t2 = t[:, :half], t[:, half:]
            out_ref[0, :, out_col:out_col + half] = ((t1 * cos - t2 * sin) * scale).astype(BF16)
            out_ref[0, :, out_col + half:out_col + RET_HEAD_QK] = (
                (t2 * cos + t1 * sin) * scale).astype(BF16)

        for h in range(RET_HEADS):
            rotary(h * RET_HEAD_QK, 1.0, q_ref, h * RET_HEAD_QK)
            rotary(RET_QK_DIM + h * RET_HEAD_QK, RET_HEAD_QK ** -0.5, k_ref, h * RET_HEAD_QK)


def _ret_proj(x, layer, g, w_in, cos, sin, *, tm=TOKEN_TILE):
    b, s, d = x.shape
    half = RET_HEAD_QK // 2
    width = 2 * RET_QK_DIM + 2 * RET_V_DIM
    nt = s // tm
    ntiles = b * nt
    tile = functools.partial(_tile_spec, tm, nt=nt, ntiles=ntiles)
    return pl.pallas_call(
        _ret_proj_kernel,
        grid=(WEIGHT_LOAD_STEPS + ntiles,),
        in_specs=[
            tile(d),
            _layer_resident((1, d), layer),
            _weight_chunk_spec((d, width), layer // 2),
            tile(half),
            tile(half),
        ],
        out_specs=[tile(RET_QK_DIM), tile(RET_QK_DIM), tile(RET_V_DIM), tile(RET_V_DIM)],
        out_shape=[
            jax.ShapeDtypeStruct((b, s, RET_QK_DIM), BF16),
            jax.ShapeDtypeStruct((b, s, RET_QK_DIM), BF16),
            jax.ShapeDtypeStruct((b, s, RET_V_DIM), BF16),
            jax.ShapeDtypeStruct((b, s, RET_V_DIM), BF16),
        ],
        scratch_shapes=[
            pltpu.VMEM((d, width), BF16),
            pltpu.VMEM((tm, d), BF16),
        ],
        compiler_params=_compiler_params(),
        name="ret_proj",
    )(x, g, w_in, cos, sin)


def _ret_scan_kernel(x_ref, q_ref, k_ref, v_ref, sg_ref, intra_ref, qd_ref, kd_ref, cd_ref,
                     gng_ref, wout32_ref, o_ref, wout_ref, state_ref, og_ref, *, tm, nt):
    n = pl.program_id(0)
    _stage_weights(n, [(wout32_ref, wout_ref)])
    tile = n - WEIGHT_LOAD_STEPS

    @pl.when(tile >= 0)
    def _():
        @pl.when(tile % nt == 0)
        def _():
            state_ref[...] = jnp.zeros_like(state_ref)

        c = RET_CHUNK
        for h in range(RET_HEADS):
            qk = slice(h * RET_HEAD_QK, (h + 1) * RET_HEAD_QK)
            vv = slice(h * RET_HEAD_V, (h + 1) * RET_HEAD_V)
            qd = qd_ref[h]
            kd = kd_ref[h]
            for i in range(tm // c):
                rows = slice(i * c, (i + 1) * c)
                qi = q_ref[0, rows, qk]
                ki = k_ref[0, rows, qk]
                vi = v_ref[0, rows, vv]
                scores = lax.dot_general(qi, ki, (((1,), (1,)), ((), ())),
                                         preferred_element_type=F32) * intra_ref[h]
                inner = jnp.dot(scores.astype(BF16), vi, preferred_element_type=F32)
                state = state_ref[h]
                cross = jnp.dot(qi, state.astype(BF16), preferred_element_type=F32) * qd
                kdec = (ki.astype(F32) * kd).astype(BF16)
                kv = lax.dot_general(kdec, vi, (((0,), (0,)), ((), ())),
                                     preferred_element_type=F32)
                state_ref[h] = cd_ref[h] * state + kv
                on = _center_scale(inner + cross) * gng_ref[:, vv]
                og_ref[rows, vv] = (sg_ref[0, rows, vv].astype(F32) * on).astype(BF16)

        o_ref[0] = x_ref[0] + jnp.dot(og_ref[...], wout_ref[...], preferred_element_type=F32)


def _decay_tables():
    c = RET_CHUNK
    log_gamma = jnp.log1p(-jnp.exp2(-5.0 - jnp.arange(RET_HEADS, dtype=F32)))
    idx = jnp.arange(c, dtype=F32)
    dist = idx[:, None] - idx[None, :]
    intra = jnp.where(dist[None] >= 0,
                      jnp.exp(log_gamma[:, None, None] * jnp.maximum(dist, 0.0)[None]), 0.0)
    q_decay = jnp.exp(log_gamma[:, None] * (idx + 1.0)[None])
    k_decay = jnp.exp(log_gamma[:, None] * (c - 1.0 - idx)[None])
    chunk_decay = jnp.exp(log_gamma * c)
    return (intra,
            jnp.broadcast_to(q_decay[:, :, None], (RET_HEADS, c, RET_HEAD_V)),
            jnp.broadcast_to(k_decay[:, :, None], (RET_HEADS, c, RET_HEAD_QK)),
            jnp.broadcast_to(chunk_decay[:, None, None], (RET_HEADS, 1, RET_HEAD_V)))


def _ret_scan(x, layer, q, k, v, sg, gn_g, w_out, *, tm=TOKEN_TILE):
    b, s, d = x.shape
    nt = s // tm
    ntiles = b * nt
    intra, qd, kd, cd = _decay_tables()
    tile = functools.partial(_tile_spec, tm, nt=nt, ntiles=ntiles)
    return pl.pallas_call(
        functools.partial(_ret_scan_kernel, tm=tm, nt=nt),
        grid=(WEIGHT_LOAD_STEPS + ntiles,),
        in_specs=[
            tile(d),
            tile(RET_QK_DIM),
            tile(RET_QK_DIM),
            tile(RET_V_DIM),
            tile(RET_V_DIM),
            _resident(intra.shape),
            _resident(qd.shape),
            _resident(kd.shape),
            _resident(cd.shape),
            _layer_resident((1, RET_V_DIM), layer // 2),
            _weight_chunk_spec((RET_V_DIM, d), layer // 2),
        ],
        out_specs=tile(d),
        out_shape=jax.ShapeDtypeStruct(x.shape, x.dtype),
        scratch_shapes=[
            pltpu.VMEM((RET_V_DIM, d), BF16),
            pltpu.VMEM((RET_HEADS, RET_HEAD_QK, RET_HEAD_V), F32),
            pltpu.VMEM((tm, RET_V_DIM), BF16),
        ],
        compiler_params=_compiler_params(),
        name="ret_scan",
    )(x, q, k, v, sg, intra, qd, kd, cd, gn_g, w_out)


def kernel(x, positions, conv_w_in, conv_dw_w, conv_dw_b, conv_ln_g, conv_ln_b, conv_w_out,
           ret_w_in, ret_gn_g, ret_w_out, ffn_w_in, ffn_dw_w, ffn_dw_b, ffn_w_out,
           norm_mix_g, norm_ffn_g, final_g):
    depth = norm_mix_g.shape[0]
    d = x.shape[-1]
    conv_dw_b, conv_ln_g, conv_ln_b = _rows(conv_dw_b), _rows(conv_ln_g), _rows(conv_ln_b)
    ret_gn_g, ffn_dw_b = _rows(ret_gn_g), _rows(ffn_dw_b)
    norm_mix_g, norm_ffn_g = _rows(norm_mix_g), _rows(norm_ffn_g)
    final_g = final_g.reshape(1, d)
    cos, sin = _rope_tables(positions)

    for i in range(depth):
        final_norm = i == depth - 1
        if i % 2 == 0:
            x = _conv_module_ffn(x, i, norm_mix_g, conv_w_in, conv_dw_w, conv_dw_b,
                                 conv_ln_g, conv_ln_b, conv_w_out,
                                 norm_ffn_g, ffn_w_in, ffn_dw_w, ffn_dw_b, ffn_w_out, final_g,
                                 final_norm=final_norm)
        else:
            q, k, v, sg = _ret_proj(x, i, norm_mix_g, ret_w_in, cos, sin)
            x = _ret_scan(x, i, q, k, v, sg, ret_gn_g, ret_w_out)
            x = _conv_ffn(x, i, norm_ffn_g, ffn_w_in, ffn_dw_w, ffn_dw_b, ffn_w_out, final_g,
                          final_norm=final_norm)
    return x
```

```python
import functools

import jax
import jax.numpy as jnp
from jax import lax
from jax.experimental import pallas as pl
from jax.experimental.pallas import tpu as pltpu

D_MODEL = 1024
CONV_WIDTH = 31
RET_HEADS = 4
RET_HEAD_QK = 256
RET_HEAD_V = 512
RET_QK_DIM = RET_HEADS * RET_HEAD_QK
RET_V_DIM = RET_HEADS * RET_HEAD_V
RET_CHUNK = 256
ROPE_BASE = 10000.0
FFN_DIM = 2816
FFN_CONV_WIDTH = 3
EPS = 1e-6

F32 = jnp.float32
BF16 = jnp.bfloat16

V7X_VMEM_BYTES = 64 * 1024 * 1024
VMEM_LIMIT_BYTES = V7X_VMEM_BYTES - 8 * 1024 * 1024
SUBLANES = 8
LANES = 128
BF16_ROW_TILE = 16

TOKEN_TILE = 512
FFN_COL_TILE = 256
CONV_HIST = 32
CONV_ROW_BLOCK = 64
CONV_COL_TILE = 256

WEIGHT_LOAD_STEPS = 4
FUSED_WEIGHT_LOAD_STEPS = 16


def _resident(shape):
    zeros = (0,) * len(shape)
    return pl.BlockSpec(shape, lambda n: zeros, pipeline_mode=pl.Buffered(1))


def _layer_resident(shape, layer):
    index = (layer,) + (0,) * len(shape)
    return pl.BlockSpec((None,) + tuple(shape), lambda n: index, pipeline_mode=pl.Buffered(1))


def _weight_chunk_spec(shape, layer, steps):
    rows, cols = shape
    assert rows % (steps * BF16_ROW_TILE) == 0, shape
    return pl.BlockSpec((None, rows // steps, cols),
                        lambda n: (layer, jnp.minimum(n, steps - 1), 0))


def _stage_weights(n, steps, pairs):
    @pl.when(n < steps)
    def _():
        for chunk_ref, dst_ref in pairs:
            rows = chunk_ref.shape[0]
            start = pl.multiple_of(n * rows, rows)
            dst_ref[pl.ds(start, rows), :] = chunk_ref[...].astype(BF16)


def _tile_spec(tm, width, nt, ntiles, steps, lag=0):
    def index(n):
        m = jnp.clip(n - steps - lag, 0, ntiles - 1)
        return (m // nt, m % nt, 0)
    return pl.BlockSpec((1, tm, width), index)


def _rows(p):
    return p.reshape(p.shape[0], 1, p.shape[1])


def _compiler_params():
    return pltpu.CompilerParams(
        dimension_semantics=("arbitrary",),
        vmem_limit_bytes=VMEM_LIMIT_BYTES,
    )


def _rmsnorm(x, g):
    ms = jnp.mean(x * x, axis=-1, keepdims=True)
    return x * lax.rsqrt(ms + EPS) * g


def _center_scale(x):
    mu = jnp.mean(x, axis=-1, keepdims=True)
    xc = x - mu
    var = jnp.mean(xc * xc, axis=-1, keepdims=True)
    return xc * lax.rsqrt(var + EPS)


def _sigmoid(x):
    return 1.0 / (1.0 + jnp.exp(-x))


def _shift_rows(a, prev, k):
    rolled = pltpu.roll(a, k, 0)
    top = rolled[:SUBLANES]
    row = lax.broadcasted_iota(jnp.int32, top.shape, 0)
    for r in range(k):
        top = jnp.where(row == r, prev[SUBLANES - k + r:SUBLANES - k + r + 1, :], top)
    return jnp.concatenate([top, rolled[SUBLANES:]], axis=0)


def _ffn_hidden(hb_ref, win_ref, dww_ref, dwb_ref, act_ref, carry_ref, *, tm, fc,
                after_chunk=None):
    for c in range(FFN_DIM // fc):
        lo = c * fc
        hb = hb_ref[...]
        a = jnp.dot(hb, win_ref[:, lo:lo + fc], preferred_element_type=F32)
        u = jnp.dot(hb, win_ref[:, FFN_DIM + lo:FFN_DIM + lo + fc],
                    preferred_element_type=F32)
        prev = carry_ref[:, lo:lo + fc]
        carry_ref[:, lo:lo + fc] = a[tm - SUBLANES:, :]
        a1 = _shift_rows(a, prev, 1)
        a2 = _shift_rows(a, prev, 2)
        conv = (dww_ref[2:3, lo:lo + fc] * a + dww_ref[1:2, lo:lo + fc] * a1
                + dww_ref[0:1, lo:lo + fc] * a2 + dwb_ref[:, lo:lo + fc])
        act_ref[:, lo:lo + fc] = (conv * _sigmoid(conv) * u).astype(BF16)
        if after_chunk is not None:
            after_chunk(c)


def _ffn_kernel(x_ref, g_ref, win32_ref, dww_ref, dwb_ref, wout32_ref, fg_ref, o_ref,
                win_ref, wout_ref, hb_ref, act_ref, carry_ref,
                *, tm, fc, nt, load_steps, final_norm):
    n = pl.program_id(0)
    _stage_weights(n, load_steps, [(win32_ref, win_ref), (wout32_ref, wout_ref)])
    tile = n - load_steps

    @pl.when(tile >= 0)
    def _():
        @pl.when(tile % nt == 0)
        def _():
            carry_ref[...] = jnp.zeros_like(carry_ref)

        x = x_ref[0]
        hb_ref[...] = _rmsnorm(x, g_ref[...]).astype(BF16)
        _ffn_hidden(hb_ref, win_ref, dww_ref, dwb_ref, act_ref, carry_ref, tm=tm, fc=fc)
        y = x + jnp.dot(act_ref[...], wout_ref[...], preferred_element_type=F32)
        o_ref[0] = _rmsnorm(y, fg_ref[...]) if final_norm else y


def _conv_ffn(x, layer, g, w_in, dw_w, dw_b, w_out, final_g, *, final_norm,
              tm=TOKEN_TILE, fc=FFN_COL_TILE):
    b, s, d = x.shape
    f = FFN_DIM
    nt = s // tm
    ntiles = b * nt
    steps = WEIGHT_LOAD_STEPS
    return pl.pallas_call(
        functools.partial(_ffn_kernel, tm=tm, fc=fc, nt=nt, load_steps=steps,
                          final_norm=final_norm),
        grid=(steps + ntiles,),
        in_specs=[
            _tile_spec(tm, d, nt, ntiles, steps),
            _layer_resident((1, d), layer),
            _weight_chunk_spec((d, 2 * f), layer, steps),
            _layer_resident((FFN_CONV_WIDTH, f), layer),
            _layer_resident((1, f), layer),
            _weight_chunk_spec((f, d), layer, steps),
            _resident((1, d)),
        ],
        out_specs=_tile_spec(tm, d, nt, ntiles, steps),
        out_shape=jax.ShapeDtypeStruct(x.shape, x.dtype),
        scratch_shapes=[
            pltpu.VMEM((d, 2 * f), BF16),
            pltpu.VMEM((f, d), BF16),
            pltpu.VMEM((tm, d), BF16),
            pltpu.VMEM((tm, f), BF16),
            pltpu.VMEM((SUBLANES, f), F32),
        ],
        compiler_params=_compiler_params(),
        name="conv_ffn",
    )(x, g, w_in, dw_w, dw_b, w_out, final_g)


def _conv_module_jobs(win_ref, dww_ref, dwb_ref, hb_ref, buf_ref, y_ref, *, tm):
    d = D_MODEL
    base = CONV_HIST - (CONV_WIDTH - 1)
    rb = CONV_ROW_BLOCK
    cw = CONV_COL_TILE

    def glu(cb):
        hb = hb_ref[...]
        a = jnp.dot(hb, win_ref[:, cb:cb + cw], preferred_element_type=F32)
        gate = jnp.dot(hb, win_ref[:, d + cb:d + cb + cw], preferred_element_type=F32)
        buf_ref[CONV_HIST:CONV_HIST + tm, cb:cb + cw] = a * _sigmoid(gate)

    def conv_block(r0, c0):
        total = None
        for phase in range(SUBLANES):
            part = None
            for j in range(CONV_WIDTH):
                off = base + j
                if off % SUBLANES != phase:
                    continue
                term = (dww_ref[j:j + 1, c0:c0 + LANES]
                        * buf_ref[r0 + off:r0 + off + rb, c0:c0 + LANES])
                part = term if part is None else part + term
            if part is not None:
                total = part if total is None else total + part
        y_ref[r0:r0 + rb, c0:c0 + LANES] = total + dwb_ref[:, c0:c0 + LANES]

    jobs = []
    for cb in range(0, d, cw):
        jobs.append(functools.partial(glu, cb))
        for c0 in range(cb, cb + cw, LANES):
            for r0 in range(0, tm, rb):
                jobs.append(functools.partial(conv_block, r0, c0))
    return jobs


def _conv_ffn_fused_kernel(x_ref, gm_ref, cwin32_ref, cdww_ref, cdwb_ref, lng_ref, lnb_ref,
                           cwout32_ref, gf_ref, fwin32_ref, fdww_ref, fdwb_ref, fwout32_ref,
                           fg_ref, o_ref,
                           cwin_ref, cwout_ref, fwin_ref, fwout_ref,
                           x1_ref, hbc_ref, buf_ref, y_ref, hb_ref, act_ref, carry_ref,
                           *, tm, fc, nt, load_steps, final_norm):
    d = D_MODEL
    n = pl.program_id(0)
    _stage_weights(n, load_steps, [(cwin32_ref, cwin_ref), (cwout32_ref, cwout_ref),
                                   (fwin32_ref, fwin_ref), (fwout32_ref, fwout_ref)])
    k = n - load_steps

    @pl.when(k >= 0)
    def _():
        conv_seq_start = k % nt == 0
        ffn_seq_start = (k - 1) % nt == 0

        @pl.when(k == 0)
        def _():
            x1_ref[...] = jnp.zeros_like(x1_ref)

        @pl.when(conv_seq_start)
        def _():
            buf_ref[0:CONV_HIST, :] = jnp.zeros((CONV_HIST, d), F32)

        @pl.when(jnp.logical_not(conv_seq_start))
        def _():
            buf_ref[0:CONV_HIST, :] = buf_ref[tm:tm + CONV_HIST, :]

        @pl.when(jnp.logical_or(ffn_seq_start, k == 0))
        def _():
            carry_ref[...] = jnp.zeros_like(carry_ref)

        x1 = x1_ref[...]
        hb_ref[...] = _rmsnorm(x1, gf_ref[...]).astype(BF16)
        x = x_ref[0]
        hbc_ref[...] = _rmsnorm(x, gm_ref[...]).astype(BF16)

        jobs = _conv_module_jobs(cwin_ref, cdww_ref, cdwb_ref, hbc_ref, buf_ref, y_ref, tm=tm)
        per_chunk = -(-len(jobs) // (FFN_DIM // fc))

        def conv_jobs_after_chunk(c):
            for job in jobs[c * per_chunk:(c + 1) * per_chunk]:
                job()

        _ffn_hidden(hb_ref, fwin_ref, fdww_ref, fdwb_ref, act_ref, carry_ref, tm=tm, fc=fc,
                    after_chunk=conv_jobs_after_chunk)
        y = x1 + jnp.dot(act_ref[...], fwout_ref[...], preferred_element_type=F32)
        o_ref[0] = _rmsnorm(y, fg_ref[...]) if final_norm else y

        yn = _center_scale(y_ref[...]) * lng_ref[...] + lnb_ref[...]
        act = (yn * _sigmoid(yn)).astype(BF16)
        x1_ref[...] = x + jnp.dot(act, cwout_ref[...], preferred_element_type=F32)


def _conv_module_ffn(x, layer, mix_g, c_w_in, c_dw_w, c_dw_b, ln_g, ln_b, c_w_out,
                     ffn_g, f_w_in, f_dw_w, f_dw_b, f_w_out, final_g, *, final_norm,
                     tm=TOKEN_TILE, fc=FFN_COL_TILE):
    b, s, d = x.shape
    f = FFN_DIM
    nt = s // tm
    ntiles = b * nt
    steps = FUSED_WEIGHT_LOAD_STEPS
    return pl.pallas_call(
        functools.partial(_conv_ffn_fused_kernel, tm=tm, fc=fc, nt=nt, load_steps=steps,
                          final_norm=final_norm),
        grid=(steps + ntiles + 1,),
        in_specs=[
            _tile_spec(tm, d, nt, ntiles, steps),
            _layer_resident((1, d), layer),
            _weight_chunk_spec((d, 2 * d), layer // 2, steps),
            _layer_resident((CONV_WIDTH, d), layer // 2),
            _layer_resident((1, d), layer // 2),
            _layer_resident((1, d), layer // 2),
            _layer_resident((1, d), layer // 2),
            _weight_chunk_spec((d, d), layer // 2, steps),
            _layer_resident((1, d), layer),
            _weight_chunk_spec((d, 2 * f), layer, steps),
            _layer_resident((FFN_CONV_WIDTH, f), layer),
            _layer_resident((1, f), layer),
            _weight_chunk_spec((f, d), layer, steps),
            _resident((1, d)),
        ],
        out_specs=_tile_spec(tm, d, nt, ntiles, steps, lag=1),
        out_shape=jax.ShapeDtypeStruct(x.shape, x.dtype),
        scratch_shapes=[
            pltpu.VMEM((d, 2 * d), BF16),
            pltpu.VMEM((d, d), BF16),
            pltpu.VMEM((d, 2 * f), BF16),
            pltpu.VMEM((f, d), BF16),
            pltpu.VMEM((tm, d), F32),
            pltpu.VMEM((tm, d), BF16),
            pltpu.VMEM((CONV_HIST + tm, d), F32),
            pltpu.VMEM((tm, d), F32),
            pltpu.VMEM((tm, d), BF16),
            pltpu.VMEM((tm, f), BF16),
            pltpu.VMEM((SUBLANES, f), F32),
        ],
        compiler_params=_compiler_params(),
        name="conv_module_ffn",
    )(x, mix_g, c_w_in, c_dw_w, c_dw_b, ln_g, ln_b, c_w_out,
      ffn_g, f_w_in, f_dw_w, f_dw_b, f_w_out, final_g)


def _rope_kernel(pos_ref, invf_ref, cos_ref, sin_ref):
    ang = pos_ref[0] * invf_ref[...]
    cos_ref[0] = jnp.cos(ang)
    sin_ref[0] = jnp.sin(ang)


def _rope_tables(positions, *, tm=TOKEN_TILE):
    b, s = positions.shape
    half = RET_HEAD_QK // 2
    inv_freq = 1.0 / (ROPE_BASE ** (jnp.arange(0, RET_HEAD_QK, 2, dtype=F32) / RET_HEAD_QK))

    def tile(width):
        return pl.BlockSpec((1, tm, width), lambda bi, i: (bi, i, 0))

    table = jax.ShapeDtypeStruct((b, s, half), F32)
    return pl.pallas_call(
        _rope_kernel,
        grid=(b, s // tm),
        in_specs=[tile(1), pl.BlockSpec((1, half), lambda bi, i: (0, 0))],
        out_specs=[tile(half), tile(half)],
        out_shape=[table, table],
        compiler_params=pltpu.CompilerParams(
            dimension_semantics=("arbitrary", "arbitrary"),
            vmem_limit_bytes=VMEM_LIMIT_BYTES,
        ),
        name="rope_tables",
    )(positions.astype(F32).reshape(b, s, 1), inv_freq.reshape(1, half))


def _ret_proj_kernel(x_ref, g_ref, win32_ref, cos_ref, sin_ref,
                     q_ref, k_ref, v_ref, sg_ref, win_ref, hb_ref, *, load_steps):
    n = pl.program_id(0)
    _stage_weights(n, load_steps, [(win32_ref, win_ref)])

    @pl.when(n >= load_steps)
    def _():
        hb_ref[...] = _rmsnorm(x_ref[0], g_ref[...]).astype(BF16)
        half = RET_HEAD_QK // 2
        cos = cos_ref[0]
        sin = sin_ref[0]
        v0 = 2 * RET_QK_DIM
        g0 = v0 + RET_V_DIM
        for h in range(RET_HEADS):
            lo = h * RET_HEAD_V
            v = jnp.dot(hb_ref[...], win_ref[:, v0 + lo:v0 + lo + RET_HEAD_V],
                        preferred_element_type=F32)
            v_ref[0, :, lo:lo + RET_HEAD_V] = v.astype(BF16)
            gt = jnp.dot(hb_ref[...], win_ref[:, g0 + lo:g0 + lo + RET_HEAD_V],
                         preferred_element_type=F32)
            sg_ref[0, :, lo:lo + RET_HEAD_V] = (gt * _sigmoid(gt)).astype(BF16)

        def rotary(col, scale, out_ref, out_col):
            t = jnp.dot(hb_ref[...], win_ref[:, col:col + RET_HEAD_QK],
                        preferred_element_type=F32)
            t1, t2 = t[:, :half], t[:, half:]
            out_ref[0, :, out_col:out_col + half] = ((t1 * cos - t2 * sin) * scale).astype(BF16)
            out_ref[0, :, out_col + half:out_col + RET_HEAD_QK] = (
                (t2 * cos + t1 * sin) * scale).astype(BF16)

        for h in range(RET_HEADS):
            rotary(h * RET_HEAD_QK, 1.0, q_ref, h * RET_HEAD_QK)
            rotary(RET_QK_DIM + h * RET_HEAD_QK, RET_HEAD_QK ** -0.5, k_ref, h * RET_HEAD_QK)


def _ret_proj(x, layer, g, w_in, cos, sin, *, tm=TOKEN_TILE):
    b, s, d = x.shape
    half = RET_HEAD_QK // 2
    width = 2 * RET_QK_DIM + 2 * RET_V_DIM
    nt = s // tm
    ntiles = b * nt
    steps = WEIGHT_LOAD_STEPS
    tile = functools.partial(_tile_spec, tm, nt=nt, ntiles=ntiles, steps=steps)
    return pl.pallas_call(
        functools.partial(_ret_proj_kernel, load_steps=steps),
        grid=(steps + ntiles,),
        in_specs=[
            tile(d),
            _layer_resident((1, d), layer),
            _weight_chunk_spec((d, width), layer // 2, steps),
            tile(half),
            tile(half),
        ],
        out_specs=[tile(RET_QK_DIM), tile(RET_QK_DIM), tile(RET_V_DIM), tile(RET_V_DIM)],
        out_shape=[
            jax.ShapeDtypeStruct((b, s, RET_QK_DIM), BF16),
            jax.ShapeDtypeStruct((b, s, RET_QK_DIM), BF16),
            jax.ShapeDtypeStruct((b, s, RET_V_DIM), BF16),
            jax.ShapeDtypeStruct((b, s, RET_V_DIM), BF16),
        ],
        scratch_shapes=[
            pltpu.VMEM((d, width), BF16),
            pltpu.VMEM((tm, d), BF16),
        ],
        compiler_params=_compiler_params(),
        name="ret_proj",
    )(x, g, w_in, cos, sin)


def _ret_scan_kernel(x_ref, q_ref, k_ref, v_ref, sg_ref, intra_ref, qd_ref, kd_ref, cd_ref,
                     gng_ref, wout32_ref, o_ref, wout_ref, state_ref, og_ref,
                     *, tm, nt, load_steps):
    n = pl.program_id(0)
    _stage_weights(n, load_steps, [(wout32_ref, wout_ref)])
    tile = n - load_steps

    @pl.when(tile >= 0)
    def _():
        @pl.when(tile % nt == 0)
        def _():
            state_ref[...] = jnp.zeros_like(state_ref)

        c = RET_CHUNK
        for h in range(RET_HEADS):
            qk = slice(h * RET_HEAD_QK, (h + 1) * RET_HEAD_QK)
            vv = slice(h * RET_HEAD_V, (h + 1) * RET_HEAD_V)
            qd = qd_ref[h]
            kd = kd_ref[h]
            for i in range(tm // c):
                rows = slice(i * c, (i + 1) * c)
                qi = q_ref[0, rows, qk]
                ki = k_ref[0, rows, qk]
                vi = v_ref[0, rows, vv]
                scores = lax.dot_general(qi, ki, (((1,), (1,)), ((), ())),
                                         preferred_element_type=F32) * intra_ref[h]
                inner = jnp.dot(scores.astype(BF16), vi, preferred_element_type=F32)
                state = state_ref[h]
                cross = jnp.dot(qi, state.astype(BF16), preferred_element_type=F32) * qd
                kdec = (ki.astype(F32) * kd).astype(BF16)
                kv = lax.dot_general(kdec, vi, (((0,), (0,)), ((), ())),
                                     preferred_element_type=F32)
                state_ref[h] = cd_ref[h] * state + kv
                on = _center_scale(inner + cross) * gng_ref[:, vv]
                og_ref[rows, vv] = (sg_ref[0, rows, vv].astype(F32) * on).astype(BF16)

        o_ref[0] = x_ref[0] + jnp.dot(og_ref[...], wout_ref[...], preferred_element_type=F32)


def _decay_tables():
    c = RET_CHUNK
    log_gamma = jnp.log1p(-jnp.exp2(-5.0 - jnp.arange(RET_HEADS, dtype=F32)))
    idx = jnp.arange(c, dtype=F32)
    dist = idx[:, None] - idx[None, :]
    intra = jnp.where(dist[None] >= 0,
                      jnp.exp(log_gamma[:, None, None] * jnp.maximum(dist, 0.0)[None]), 0.0)
    q_decay = jnp.exp(log_gamma[:, None] * (idx + 1.0)[None])
    k_decay = jnp.exp(log_gamma[:, None] * (c - 1.0 - idx)[None])
    chunk_decay = jnp.exp(log_gamma * c)
    return (intra,
            jnp.broadcast_to(q_decay[:, :, None], (RET_HEADS, c, RET_HEAD_V)),
            jnp.broadcast_to(k_decay[:, :, None], (RET_HEADS, c, RET_HEAD_QK)),
            jnp.broadcast_to(chunk_decay[:, None, None], (RET_HEADS, 1, RET_HEAD_V)))


def _ret_scan(x, layer, q, k, v, sg, gn_g, w_out, *, tm=TOKEN_TILE):
    b, s, d = x.shape
    nt = s // tm
    ntiles = b * nt
    intra, qd, kd, cd = _decay_tables()
    steps = WEIGHT_LOAD_STEPS
    tile = functools.partial(_tile_spec, tm, nt=nt, ntiles=ntiles, steps=steps)
    return pl.pallas_call(
        functools.partial(_ret_scan_kernel, tm=tm, nt=nt, load_steps=steps),
        grid=(steps + ntiles,),
        in_specs=[
            tile(d),
            tile(RET_QK_DIM),
            tile(RET_QK_DIM),
            tile(RET_V_DIM),
            tile(RET_V_DIM),
            _resident(intra.shape),
            _resident(qd.shape),
            _resident(kd.shape),
            _resident(cd.shape),
            _layer_resident((1, RET_V_DIM), layer // 2),
            _weight_chunk_spec((RET_V_DIM, d), layer // 2, steps),
        ],
        out_specs=tile(d),
        out_shape=jax.ShapeDtypeStruct(x.shape, x.dtype),
        scratch_shapes=[
            pltpu.VMEM((RET_V_DIM, d), BF16),
            pltpu.VMEM((RET_HEADS, RET_HEAD_QK, RET_HEAD_V), F32),
            pltpu.VMEM((tm, RET_V_DIM), BF16),
        ],
        compiler_params=_compiler_params(),
        name="ret_scan",
    )(x, q, k, v, sg, intra, qd, kd, cd, gn_g, w_out)


def kernel(x, positions, conv_w_in, conv_dw_w, conv_dw_b, conv_ln_g, conv_ln_b, conv_w_out,
           ret_w_in, ret_gn_g, ret_w_out, ffn_w_in, ffn_dw_w, ffn_dw_b, ffn_w_out,
           norm_mix_g, norm_ffn_g, final_g):
    depth = norm_mix_g.shape[0]
    d = x.shape[-1]
    conv_dw_b, conv_ln_g, conv_ln_b = _rows(conv_dw_b), _rows(conv_ln_g), _rows(conv_ln_b)
    ret_gn_g, ffn_dw_b = _rows(ret_gn_g), _rows(ffn_dw_b)
    norm_mix_g, norm_ffn_g = _rows(norm_mix_g), _rows(norm_ffn_g)
    final_g = final_g.reshape(1, d)
    cos, sin = _rope_tables(positions)

    for i in range(depth):
        final_norm = i == depth - 1
        if i % 2 == 0:
            x = _conv_module_ffn(x, i, norm_mix_g, conv_w_in, conv_dw_w, conv_dw_b,
                                 conv_ln_g, conv_ln_b, conv_w_out,
                                 norm_ffn_g, ffn_w_in, ffn_dw_w, ffn_dw_b, ffn_w_out, final_g,
                                 final_norm=final_norm)
        else:
            q, k, v, sg = _ret_proj(x, i, norm_mix_g, ret_w_in, cos, sin)
            x = _ret_scan(x, i, q, k, v, sg, ret_gn_g, ret_w_out)
            x = _conv_ffn(x, i, norm_ffn_g, ffn_w_in, ffn_dw_w, ffn_dw_b, ffn_w_out, final_g,
                          final_norm=final_norm)
    return x
```

```python
import functools

import jax
import jax.numpy as jnp
from jax import lax
from jax.experimental import pallas as pl
from jax.experimental.pallas import tpu as pltpu

D_MODEL = 1024
CONV_WIDTH = 31
RET_HEADS = 4
RET_HEAD_QK = 256
RET_HEAD_V = 512
RET_QK_DIM = RET_HEADS * RET_HEAD_QK
RET_V_DIM = RET_HEADS * RET_HEAD_V
RET_CHUNK = 256
ROPE_BASE = 10000.0
FFN_DIM = 2816
FFN_CONV_WIDTH = 3
EPS = 1e-6

F32 = jnp.float32
BF16 = jnp.bfloat16

V7X_VMEM_BYTES = 64 * 1024 * 1024
VMEM_LIMIT_BYTES = V7X_VMEM_BYTES - 8 * 1024 * 1024
SUBLANES = 8
LANES = 128
BF16_ROW_TILE = 16

TOKEN_TILE = 512
FFN_COL_TILE = 256
CONV_HIST = 32
CONV_ROW_BLOCK = 64
CONV_COL_TILE = 256

WEIGHT_LOAD_STEPS = 4
FFN_WEIGHT_LOAD_STEPS = 8
FUSED_WEIGHT_LOAD_STEPS = 16
FFN_TOKEN_TILE = 1024


def _resident(shape):
    zeros = (0,) * len(shape)
    return pl.BlockSpec(shape, lambda n: zeros, pipeline_mode=pl.Buffered(1))


def _layer_resident(shape, layer):
    index = (layer,) + (0,) * len(shape)
    return pl.BlockSpec((None,) + tuple(shape), lambda n: index, pipeline_mode=pl.Buffered(1))


def _weight_chunk_spec(shape, layer, steps):
    rows, cols = shape
    assert rows % (steps * BF16_ROW_TILE) == 0, shape
    return pl.BlockSpec((None, rows // steps, cols),
                        lambda n: (layer, jnp.minimum(n, steps - 1), 0))


def _stage_weights(n, steps, pairs):
    @pl.when(n < steps)
    def _():
        for chunk_ref, dst_ref in pairs:
            rows = chunk_ref.shape[0]
            start = pl.multiple_of(n * rows, rows)
            dst_ref[pl.ds(start, rows), :] = chunk_ref[...].astype(BF16)


def _tile_spec(tm, width, nt, ntiles, steps, lag=0):
    def index(n):
        m = jnp.clip(n - steps - lag, 0, ntiles - 1)
        return (m // nt, m % nt, 0)
    return pl.BlockSpec((1, tm, width), index)


def _rows(p):
    return p.reshape(p.shape[0], 1, p.shape[1])


def _compiler_params():
    return pltpu.CompilerParams(
        dimension_semantics=("arbitrary",),
        vmem_limit_bytes=VMEM_LIMIT_BYTES,
    )


def _rmsnorm(x, g):
    ms = jnp.mean(x * x, axis=-1, keepdims=True)
    return x * lax.rsqrt(ms + EPS) * g


def _center_scale(x):
    mu = jnp.mean(x, axis=-1, keepdims=True)
    xc = x - mu
    var = jnp.mean(xc * xc, axis=-1, keepdims=True)
    return xc * lax.rsqrt(var + EPS)


def _sigmoid(x):
    return 1.0 / (1.0 + jnp.exp(-x))


def _shift_rows(a, prev, k):
    rolled = pltpu.roll(a, k, 0)
    top = rolled[:SUBLANES]
    row = lax.broadcasted_iota(jnp.int32, top.shape, 0)
    for r in range(k):
        top = jnp.where(row == r, prev[SUBLANES - k + r:SUBLANES - k + r + 1, :], top)
    return jnp.concatenate([top, rolled[SUBLANES:]], axis=0)


def _ffn_hidden(hb_ref, win_ref, dww_ref, dwb_ref, act_ref, carry_ref, *, tm, fc,
                after_chunk=None):
    for c in range(FFN_DIM // fc):
        lo = c * fc
        hb = hb_ref[...]
        a = jnp.dot(hb, win_ref[:, lo:lo + fc], preferred_element_type=F32)
        u = jnp.dot(hb, win_ref[:, FFN_DIM + lo:FFN_DIM + lo + fc],
                    preferred_element_type=F32)
        prev = carry_ref[:, lo:lo + fc]
        carry_ref[:, lo:lo + fc] = a[tm - SUBLANES:, :]
        a1 = _shift_rows(a, prev, 1)
        a2 = _shift_rows(a, prev, 2)
        conv = (dww_ref[2:3, lo:lo + fc] * a + dww_ref[1:2, lo:lo + fc] * a1
                + dww_ref[0:1, lo:lo + fc] * a2 + dwb_ref[:, lo:lo + fc])
        act_ref[:, lo:lo + fc] = (conv * _sigmoid(conv) * u).astype(BF16)
        if after_chunk is not None:
            after_chunk(c)


def _ffn_kernel(x_ref, g_ref, win32_ref, dww_ref, dwb_ref, wout32_ref, fg_ref, o_ref,
                win_ref, wout_ref, hb_ref, act_ref, carry_ref,
                *, tm, fc, nt, load_steps, final_norm):
    n = pl.program_id(0)
    _stage_weights(n, load_steps, [(win32_ref, win_ref), (wout32_ref, wout_ref)])
    tile = n - load_steps

    @pl.when(tile >= 0)
    def _():
        @pl.when(tile % nt == 0)
        def _():
            carry_ref[...] = jnp.zeros_like(carry_ref)

        x = x_ref[0]
        hb_ref[...] = _rmsnorm(x, g_ref[...]).astype(BF16)
        _ffn_hidden(hb_ref, win_ref, dww_ref, dwb_ref, act_ref, carry_ref, tm=tm, fc=fc)
        y = x + jnp.dot(act_ref[...], wout_ref[...], preferred_element_type=F32)
        o_ref[0] = _rmsnorm(y, fg_ref[...]) if final_norm else y


def _conv_ffn(x, layer, g, w_in, dw_w, dw_b, w_out, final_g, *, final_norm,
              tm=FFN_TOKEN_TILE, fc=FFN_COL_TILE):
    b, s, d = x.shape
    f = FFN_DIM
    nt = s // tm
    ntiles = b * nt
    steps = FFN_WEIGHT_LOAD_STEPS
    return pl.pallas_call(
        functools.partial(_ffn_kernel, tm=tm, fc=fc, nt=nt, load_steps=steps,
                          final_norm=final_norm),
        grid=(steps + ntiles,),
        in_specs=[
            _tile_spec(tm, d, nt, ntiles, steps),
            _layer_resident((1, d), layer),
            _weight_chunk_spec((d, 2 * f), layer, steps),
            _layer_resident((FFN_CONV_WIDTH, f), layer),
            _layer_resident((1, f), layer),
            _weight_chunk_spec((f, d), layer, steps),
            _resident((1, d)),
        ],
        out_specs=_tile_spec(tm, d, nt, ntiles, steps),
        out_shape=jax.ShapeDtypeStruct(x.shape, x.dtype),
        scratch_shapes=[
            pltpu.VMEM((d, 2 * f), BF16),
            pltpu.VMEM((f, d), BF16),
            pltpu.VMEM((tm, d), BF16),
            pltpu.VMEM((tm, f), BF16),
            pltpu.VMEM((SUBLANES, f), F32),
        ],
        compiler_params=_compiler_params(),
        name="conv_ffn",
    )(x, g, w_in, dw_w, dw_b, w_out, final_g)


def _conv_module_jobs(win_ref, dww_ref, dwb_ref, hb_ref, buf_ref, y_ref, *, tm):
    d = D_MODEL
    base = CONV_HIST - (CONV_WIDTH - 1)
    rb = CONV_ROW_BLOCK
    cw = CONV_COL_TILE

    def glu(cb):
        hb = hb_ref[...]
        a = jnp.dot(hb, win_ref[:, cb:cb + cw], preferred_element_type=F32)
        gate = jnp.dot(hb, win_ref[:, d + cb:d + cb + cw], preferred_element_type=F32)
        buf_ref[CONV_HIST:CONV_HIST + tm, cb:cb + cw] = a * _sigmoid(gate)

    def conv_block(r0, c0):
        total = None
        for phase in range(SUBLANES):
            part = None
            for j in range(CONV_WIDTH):
                off = base + j
                if off % SUBLANES != phase:
                    continue
                term = (dww_ref[j:j + 1, c0:c0 + LANES]
                        * buf_ref[r0 + off:r0 + off + rb, c0:c0 + LANES])
                part = term if part is None else part + term
            if part is not None:
                total = part if total is None else total + part
        y_ref[r0:r0 + rb, c0:c0 + LANES] = total + dwb_ref[:, c0:c0 + LANES]

    jobs = []
    for cb in range(0, d, cw):
        jobs.append(functools.partial(glu, cb))
        for c0 in range(cb, cb + cw, LANES):
            for r0 in range(0, tm, rb):
                jobs.append(functools.partial(conv_block, r0, c0))
    return jobs


def _conv_ffn_fused_kernel(x_ref, gm_ref, cwin32_ref, cdww_ref, cdwb_ref, lng_ref, lnb_ref,
                           cwout32_ref, gf_ref, fwin32_ref, fdww_ref, fdwb_ref, fwout32_ref,
                           fg_ref, o_ref,
                           cwin_ref, cwout_ref, fwin_ref, fwout_ref,
                           x1_ref, hbc_ref, buf_ref, y_ref, hb_ref, act_ref, carry_ref,
                           *, tm, fc, nt, load_steps, final_norm):
    d = D_MODEL
    n = pl.program_id(0)
    _stage_weights(n, load_steps, [(cwin32_ref, cwin_ref), (cwout32_ref, cwout_ref),
                                   (fwin32_ref, fwin_ref), (fwout32_ref, fwout_ref)])
    k = n - load_steps

    @pl.when(k >= 0)
    def _():
        conv_seq_start = k % nt == 0
        ffn_seq_start = (k - 1) % nt == 0

        @pl.when(k == 0)
        def _():
            x1_ref[...] = jnp.zeros_like(x1_ref)

        @pl.when(conv_seq_start)
        def _():
            buf_ref[0:CONV_HIST, :] = jnp.zeros((CONV_HIST, d), F32)

        @pl.when(jnp.logical_not(conv_seq_start))
        def _():
            buf_ref[0:CONV_HIST, :] = buf_ref[tm:tm + CONV_HIST, :]

        @pl.when(jnp.logical_or(ffn_seq_start, k == 0))
        def _():
            carry_ref[...] = jnp.zeros_like(carry_ref)

        x1 = x1_ref[...]
        hb_ref[...] = _rmsnorm(x1, gf_ref[...]).astype(BF16)
        x = x_ref[0]
        hbc_ref[...] = _rmsnorm(x, gm_ref[...]).astype(BF16)

        jobs = _conv_module_jobs(cwin_ref, cdww_ref, cdwb_ref, hbc_ref, buf_ref, y_ref, tm=tm)
        per_chunk = -(-len(jobs) // (FFN_DIM // fc))

        def conv_jobs_after_chunk(c):
            for job in jobs[c * per_chunk:(c + 1) * per_chunk]:
                job()

        _ffn_hidden(hb_ref, fwin_ref, fdww_ref, fdwb_ref, act_ref, carry_ref, tm=tm, fc=fc,
                    after_chunk=conv_jobs_after_chunk)
        y = x1 + jnp.dot(act_ref[...], fwout_ref[...], preferred_element_type=F32)
        o_ref[0] = _rmsnorm(y, fg_ref[...]) if final_norm else y

        yn = _center_scale(y_ref[...]) * lng_ref[...] + lnb_ref[...]
        act = (yn * _sigmoid(yn)).astype(BF16)
        x1_ref[...] = x + jnp.dot(act, cwout_ref[...], preferred_element_type=F32)


def _conv_module_ffn(x, layer, mix_g, c_w_in, c_dw_w, c_dw_b, ln_g, ln_b, c_w_out,
                     ffn_g, f_w_in, f_dw_w, f_dw_b, f_w_out, final_g, *, final_norm,
                     tm=TOKEN_TILE, fc=FFN_COL_TILE):
    b, s, d = x.shape
    f = FFN_DIM
    nt = s // tm
    ntiles = b * nt
    steps = FUSED_WEIGHT_LOAD_STEPS
    return pl.pallas_call(
        functools.partial(_conv_ffn_fused_kernel, tm=tm, fc=fc, nt=nt, load_steps=steps,
                          final_norm=final_norm),
        grid=(steps + ntiles + 1,),
        in_specs=[
            _tile_spec(tm, d, nt, ntiles, steps),
            _layer_resident((1, d), layer),
            _weight_chunk_spec((d, 2 * d), layer // 2, steps),
            _layer_resident((CONV_WIDTH, d), layer // 2),
            _layer_resident((1, d), layer // 2),
            _layer_resident((1, d), layer // 2),
            _layer_resident((1, d), layer // 2),
            _weight_chunk_spec((d, d), layer // 2, steps),
            _layer_resident((1, d), layer),
            _weight_chunk_spec((d, 2 * f), layer, steps),
            _layer_resident((FFN_CONV_WIDTH, f), layer),
            _layer_resident((1, f), layer),
            _weight_chunk_spec((f, d), layer, steps),
            _resident((1, d)),
        ],
        out_specs=_tile_spec(tm, d, nt, ntiles, steps, lag=1),
        out_shape=jax.ShapeDtypeStruct(x.shape, x.dtype),
        scratch_shapes=[
            pltpu.VMEM((d, 2 * d), BF16),
            pltpu.VMEM((d, d), BF16),
            pltpu.VMEM((d, 2 * f), BF16),
            pltpu.VMEM((f, d), BF16),
            pltpu.VMEM((tm, d), F32),
            pltpu.VMEM((tm, d), BF16),
            pltpu.VMEM((CONV_HIST + tm, d), F32),
            pltpu.VMEM((tm, d), F32),
            pltpu.VMEM((tm, d), BF16),
            pltpu.VMEM((tm, f), BF16),
            pltpu.VMEM((SUBLANES, f), F32),
        ],
        compiler_params=_compiler_params(),
        name="conv_module_ffn",
    )(x, mix_g, c_w_in, c_dw_w, c_dw_b, ln_g, ln_b, c_w_out,
      ffn_g, f_w_in, f_dw_w, f_dw_b, f_w_out, final_g)


def _rope_kernel(pos_ref, invf_ref, cos_ref, sin_ref):
    ang = pos_ref[0] * invf_ref[...]
    cos_ref[0] = jnp.cos(ang)
    sin_ref[0] = jnp.sin(ang)


def _rope_tables(positions, *, tm=TOKEN_TILE):
    b, s = positions.shape
    half = RET_HEAD_QK // 2
    inv_freq = 1.0 / (ROPE_BASE ** (jnp.arange(0, RET_HEAD_QK, 2, dtype=F32) / RET_HEAD_QK))

    def tile(width):
        return pl.BlockSpec((1, tm, width), lambda bi, i: (bi, i, 0))

    table = jax.ShapeDtypeStruct((b, s, half), F32)
    return pl.pallas_call(
        _rope_kernel,
        grid=(b, s // tm),
        in_specs=[tile(1), pl.BlockSpec((1, half), lambda bi, i: (0, 0))],
        out_specs=[tile(half), tile(half)],
        out_shape=[table, table],
        compiler_params=pltpu.CompilerParams(
            dimension_semantics=("arbitrary", "arbitrary"),
            vmem_limit_bytes=VMEM_LIMIT_BYTES,
        ),
        name="rope_tables",
    )(positions.astype(F32).reshape(b, s, 1), inv_freq.reshape(1, half))


def _ret_proj_kernel(x_ref, g_ref, win32_ref, cos_ref, sin_ref,
                     q_ref, k_ref, v_ref, sg_ref, win_ref, hb_ref, *, load_steps):
    n = pl.program_id(0)
    _stage_weights(n, load_steps, [(win32_ref, win_ref)])

    @pl.when(n >= load_steps)
    def _():
        hb_ref[...] = _rmsnorm(x_ref[0], g_ref[...]).astype(BF16)
        half = RET_HEAD_QK // 2
        cos = cos_ref[0]
        sin = sin_ref[0]
        v0 = 2 * RET_QK_DIM
        g0 = v0 + RET_V_DIM
        for h in range(RET_HEADS):
            lo = h * RET_HEAD_V
            v = jnp.dot(hb_ref[...], win_ref[:, v0 + lo:v0 + lo + RET_HEAD_V],
                        preferred_element_type=F32)
            v_ref[0, :, lo:lo + RET_HEAD_V] = v.astype(BF16)
            gt = jnp.dot(hb_ref[...], win_ref[:, g0 + lo:g0 + lo + RET_HEAD_V],
                         preferred_element_type=F32)
            sg_ref[0, :, lo:lo + RET_HEAD_V] = (gt * _sigmoid(gt)).astype(BF16)

        def rotary(col, scale, out_ref, out_col):
            t = jnp.dot(hb_ref[...], win_ref[:, col:col + RET_HEAD_QK],
                        preferred_element_type=F32)
            t1, t2 = t[:, :half], t[:, half:]
            out_ref[0, :, out_col:out_col + half] = ((t1 * cos - t2 * sin) * scale).astype(BF16)
            out_ref[0, :, out_col + half:out_col + RET_HEAD_QK] = (
                (t2 * cos + t1 * sin) * scale).astype(BF16)

        for h in range(RET_HEADS):
            rotary(h * RET_HEAD_QK, 1.0, q_ref, h * RET_HEAD_QK)
            rotary(RET_QK_DIM + h * RET_HEAD_QK, RET_HEAD_QK ** -0.5, k_ref, h * RET_HEAD_QK)


def _ret_proj(x, layer, g, w_in, cos, sin, *, tm=TOKEN_TILE):
    b, s, d = x.shape
    half = RET_HEAD_QK // 2
    width = 2 * RET_QK_DIM + 2 * RET_V_DIM
    nt = s // tm
    ntiles = b * nt
    steps = WEIGHT_LOAD_STEPS
    tile = functools.partial(_tile_spec, tm, nt=nt, ntiles=ntiles, steps=steps)
    return pl.pallas_call(
        functools.partial(_ret_proj_kernel, load_steps=steps),
        grid=(steps + ntiles,),
        in_specs=[
            tile(d),
            _layer_resident((1, d), layer),
            _weight_chunk_spec((d, width), layer // 2, steps),
            tile(half),
            tile(half),
        ],
        out_specs=[tile(RET_QK_DIM), tile(RET_QK_DIM), tile(RET_V_DIM), tile(RET_V_DIM)],
        out_shape=[
            jax.ShapeDtypeStruct((b, s, RET_QK_DIM), BF16),
            jax.ShapeDtypeStruct((b, s, RET_QK_DIM), BF16),
            jax.ShapeDtypeStruct((b, s, RET_V_DIM), BF16),
            jax.ShapeDtypeStruct((b, s, RET_V_DIM), BF16),
        ],
        scratch_shapes=[
            pltpu.VMEM((d, width), BF16),
            pltpu.VMEM((tm, d), BF16),
        ],
        compiler_params=_compiler_params(),
        name="ret_proj",
    )(x, g, w_in, cos, sin)


def _ret_scan_kernel(x_ref, q_ref, k_ref, v_ref, sg_ref, intra_ref, qd_ref, kd_ref, cd_ref,
                     gng_ref, wout32_ref, o_ref, wout_ref, state_ref, og_ref,
                     *, tm, nt, load_steps):
    n = pl.program_id(0)
    _stage_weights(n, load_steps, [(wout32_ref, wout_ref)])
    tile = n - load_steps

    @pl.when(tile >= 0)
    def _():
        @pl.when(tile % nt == 0)
        def _():
            state_ref[...] = jnp.zeros_like(state_ref)

        c = RET_CHUNK
        for h in range(RET_HEADS):
            qk = slice(h * RET_HEAD_QK, (h + 1) * RET_HEAD_QK)
            vv = slice(h * RET_HEAD_V, (h + 1) * RET_HEAD_V)
            qd = qd_ref[h]
            kd = kd_ref[h]
            for i in range(tm // c):
                rows = slice(i * c, (i + 1) * c)
                qi = q_ref[0, rows, qk]
                ki = k_ref[0, rows, qk]
                vi = v_ref[0, rows, vv]
                scores = lax.dot_general(qi, ki, (((1,), (1,)), ((), ())),
                                         preferred_element_type=F32) * intra_ref[h]
                inner = jnp.dot(scores.astype(BF16), vi, preferred_element_type=F32)
                state = state_ref[h]
                cross = jnp.dot(qi, state.astype(BF16), preferred_element_type=F32) * qd
                kdec = (ki.astype(F32) * kd).astype(BF16)
                kv = lax.dot_general(kdec, vi, (((0,), (0,)), ((), ())),
                                     preferred_element_type=F32)
                state_ref[h] = cd_ref[h] * state + kv
                on = _center_scale(inner + cross) * gng_ref[:, vv]
                og_ref[rows, vv] = (sg_ref[0, rows, vv].astype(F32) * on).astype(BF16)

        o_ref[0] = x_ref[0] + jnp.dot(og_ref[...], wout_ref[...], preferred_element_type=F32)


def _decay_tables():
    c = RET_CHUNK
    log_gamma = jnp.log1p(-jnp.exp2(-5.0 - jnp.arange(RET_HEADS, dtype=F32)))
    idx = jnp.arange(c, dtype=F32)
    dist = idx[:, None] - idx[None, :]
    intra = jnp.where(dist[None] >= 0,
                      jnp.exp(log_gamma[:, None, None] * jnp.maximum(dist, 0.0)[None]), 0.0)
    q_decay = jnp.exp(log_gamma[:, None] * (idx + 1.0)[None])
    k_decay = jnp.exp(log_gamma[:, None] * (c - 1.0 - idx)[None])
    chunk_decay = jnp.exp(log_gamma * c)
    return (intra,
            jnp.broadcast_to(q_decay[:, :, None], (RET_HEADS, c, RET_HEAD_V)),
            jnp.broadcast_to(k_decay[:, :, None], (RET_HEADS, c, RET_HEAD_QK)),
            jnp.broadcast_to(chunk_decay[:, None, None], (RET_HEADS, 1, RET_HEAD_V)))


def _ret_scan(x, layer, q, k, v, sg, gn_g, w_out, *, tm=TOKEN_TILE):
    b, s, d = x.shape
    nt = s // tm
    ntiles = b * nt
    intra, qd, kd, cd = _decay_tables()
    steps = WEIGHT_LOAD_STEPS
    tile = functools.partial(_tile_spec, tm, nt=nt, ntiles=ntiles, steps=steps)
    return pl.pallas_call(
        functools.partial(_ret_scan_kernel, tm=tm, nt=nt, load_steps=steps),
        grid=(steps + ntiles,),
        in_specs=[
            tile(d),
            tile(RET_QK_DIM),
            tile(RET_QK_DIM),
            tile(RET_V_DIM),
            tile(RET_V_DIM),
            _resident(intra.shape),
            _resident(qd.shape),
            _resident(kd.shape),
            _resident(cd.shape),
            _layer_resident((1, RET_V_DIM), layer // 2),
            _weight_chunk_spec((RET_V_DIM, d), layer // 2, steps),
        ],
        out_specs=tile(d),
        out_shape=jax.ShapeDtypeStruct(x.shape, x.dtype),
        scratch_shapes=[
            pltpu.VMEM((RET_V_DIM, d), BF16),
            pltpu.VMEM((RET_HEADS, RET_HEAD_QK, RET_HEAD_V), F32),
            pltpu.VMEM((tm, RET_V_DIM), BF16),
        ],
        compiler_params=_compiler_params(),
        name="ret_scan",
    )(x, q, k, v, sg, intra, qd, kd, cd, gn_g, w_out)


def kernel(x, positions, conv_w_in, conv_dw_w, conv_dw_b, conv_ln_g, conv_ln_b, conv_w_out,
           ret_w_in, ret_gn_g, ret_w_out, ffn_w_in, ffn_dw_w, ffn_dw_b, ffn_w_out,
           norm_mix_g, norm_ffn_g, final_g):
    depth = norm_mix_g.shape[0]
    d = x.shape[-1]
    conv_dw_b, conv_ln_g, conv_ln_b = _rows(conv_dw_b), _rows(conv_ln_g), _rows(conv_ln_b)
    ret_gn_g, ffn_dw_b = _rows(ret_gn_g), _rows(ffn_dw_b)
    norm_mix_g, norm_ffn_g = _rows(norm_mix_g), _rows(norm_ffn_g)
    final_g = final_g.reshape(1, d)
    cos, sin = _rope_tables(positions)

    for i in range(depth):
        final_norm = i == depth - 1
        if i % 2 == 0:
            x = _conv_module_ffn(x, i, norm_mix_g, conv_w_in, conv_dw_w, conv_dw_b,
                                 conv_ln_g, conv_ln_b, conv_w_out,
                                 norm_ffn_g, ffn_w_in, ffn_dw_w, ffn_dw_b, ffn_w_out, final_g,
                                 final_norm=final_norm)
        else:
            q, k, v, sg = _ret_proj(x, i, norm_mix_g, ret_w_in, cos, sin)
            x = _ret_scan(x, i, q, k, v, sg, ret_gn_g, ret_w_out)
            x = _conv_ffn(x, i, norm_ffn_g, ffn_w_in, ffn_dw_w, ffn_dw_b, ffn_w_out, final_g,
                          final_norm=final_norm)
    return x
```

```python
import functools

import jax
import jax.numpy as jnp
from jax import lax
from jax.experimental import pallas as pl
from jax.experimental.pallas import tpu as pltpu

D_MODEL = 1024
CONV_WIDTH = 31
RET_HEADS = 4
RET_HEAD_QK = 256
RET_HEAD_V = 512
RET_QK_DIM = RET_HEADS * RET_HEAD_QK
RET_V_DIM = RET_HEADS * RET_HEAD_V
RET_CHUNK = 256
ROPE_BASE = 10000.0
FFN_DIM = 2816
FFN_CONV_WIDTH = 3
EPS = 1e-6

F32 = jnp.float32
BF16 = jnp.bfloat16

V7X_VMEM_BYTES = 64 * 1024 * 1024
VMEM_LIMIT_BYTES = V7X_VMEM_BYTES - 8 * 1024 * 1024
SUBLANES = 8
LANES = 128
BF16_ROW_TILE = 16

TOKEN_TILE = 512
FFN_COL_TILE = 256
CONV_HIST = 32
CONV_ROW_BLOCK = 64
CONV_COL_TILE = 256
RET_PROJ_COL_TILE = 256

WEIGHT_LOAD_STEPS = 4
FFN_WEIGHT_LOAD_STEPS = 8
FUSED_WEIGHT_LOAD_STEPS = 16
FFN_TOKEN_TILE = 1024


def _resident(shape):
    zeros = (0,) * len(shape)
    return pl.BlockSpec(shape, lambda n: zeros, pipeline_mode=pl.Buffered(1))


def _layer_resident(shape, layer):
    index = (layer,) + (0,) * len(shape)
    return pl.BlockSpec((None,) + tuple(shape), lambda n: index, pipeline_mode=pl.Buffered(1))


def _weight_chunk_spec(shape, layer, steps):
    rows, cols = shape
    assert rows % (steps * BF16_ROW_TILE) == 0, shape
    return pl.BlockSpec((None, rows // steps, cols),
                        lambda n: (layer, jnp.minimum(n, steps - 1), 0))


def _stage_weights(n, steps, pairs):
    @pl.when(n < steps)
    def _():
        for chunk_ref, dst_ref in pairs:
            rows = chunk_ref.shape[0]
            start = pl.multiple_of(n * rows, rows)
            dst_ref[pl.ds(start, rows), :] = chunk_ref[...].astype(BF16)


def _tile_spec(tm, width, nt, ntiles, steps, lag=0):
    def index(n):
        m = jnp.clip(n - steps - lag, 0, ntiles - 1)
        return (m // nt, m % nt, 0)
    return pl.BlockSpec((1, tm, width), index)


def _rows(p):
    return p.reshape(p.shape[0], 1, p.shape[1])


def _compiler_params():
    return pltpu.CompilerParams(
        dimension_semantics=("arbitrary",),
        vmem_limit_bytes=VMEM_LIMIT_BYTES,
    )


def _rmsnorm(x, g):
    ms = jnp.mean(x * x, axis=-1, keepdims=True)
    return x * lax.rsqrt(ms + EPS) * g


def _center_scale(x):
    mu = jnp.mean(x, axis=-1, keepdims=True)
    xc = x - mu
    var = jnp.mean(xc * xc, axis=-1, keepdims=True)
    return xc * lax.rsqrt(var + EPS)


def _sigmoid(x):
    return 1.0 / (1.0 + jnp.exp(-x))


def _shift_rows(a, prev, k):
    rolled = pltpu.roll(a, k, 0)
    top = rolled[:SUBLANES]
    row = lax.broadcasted_iota(jnp.int32, top.shape, 0)
    for r in range(k):
        top = jnp.where(row == r, prev[SUBLANES - k + r:SUBLANES - k + r + 1, :], top)
    return jnp.concatenate([top, rolled[SUBLANES:]], axis=0)


def _ffn_hidden(hb_ref, win_ref, dww_ref, dwb_ref, act_ref, carry_ref, *, tm, fc,
                after_chunk=None):
    for c in range(FFN_DIM // fc):
        lo = c * fc
        hb = hb_ref[...]
        a = jnp.dot(hb, win_ref[:, lo:lo + fc], preferred_element_type=F32)
        u = jnp.dot(hb, win_ref[:, FFN_DIM + lo:FFN_DIM + lo + fc],
                    preferred_element_type=F32)
        prev = carry_ref[:, lo:lo + fc]
        carry_ref[:, lo:lo + fc] = a[tm - SUBLANES:, :]
        a1 = _shift_rows(a, prev, 1)
        a2 = _shift_rows(a, prev, 2)
        conv = (dww_ref[2:3, lo:lo + fc] * a + dww_ref[1:2, lo:lo + fc] * a1
                + dww_ref[0:1, lo:lo + fc] * a2 + dwb_ref[:, lo:lo + fc])
        act_ref[:, lo:lo + fc] = (conv * _sigmoid(conv) * u).astype(BF16)
        if after_chunk is not None:
            after_chunk(c)


def _ffn_kernel(x_ref, g_ref, win32_ref, dww_ref, dwb_ref, wout32_ref, fg_ref, o_ref,
                win_ref, wout_ref, hb_ref, act_ref, carry_ref,
                *, tm, fc, nt, load_steps, final_norm):
    n = pl.program_id(0)
    _stage_weights(n, load_steps, [(win32_ref, win_ref), (wout32_ref, wout_ref)])
    tile = n - load_steps

    @pl.when(tile >= 0)
    def _():
        @pl.when(tile % nt == 0)
        def _():
            carry_ref[...] = jnp.zeros_like(carry_ref)

        x = x_ref[0]
        hb_ref[...] = _rmsnorm(x, g_ref[...]).astype(BF16)
        _ffn_hidden(hb_ref, win_ref, dww_ref, dwb_ref, act_ref, carry_ref, tm=tm, fc=fc)
        y = x + jnp.dot(act_ref[...], wout_ref[...], preferred_element_type=F32)
        o_ref[0] = _rmsnorm(y, fg_ref[...]) if final_norm else y


def _conv_ffn(x, layer, g, w_in, dw_w, dw_b, w_out, final_g, *, final_norm,
              tm=FFN_TOKEN_TILE, fc=FFN_COL_TILE):
    b, s, d = x.shape
    f = FFN_DIM
    nt = s // tm
    ntiles = b * nt
    steps = FFN_WEIGHT_LOAD_STEPS
    return pl.pallas_call(
        functools.partial(_ffn_kernel, tm=tm, fc=fc, nt=nt, load_steps=steps,
                          final_norm=final_norm),
        grid=(steps + ntiles,),
        in_specs=[
            _tile_spec(tm, d, nt, ntiles, steps),
            _layer_resident((1, d), layer),
            _weight_chunk_spec((d, 2 * f), layer, steps),
            _layer_resident((FFN_CONV_WIDTH, f), layer),
            _layer_resident((1, f), layer),
            _weight_chunk_spec((f, d), layer, steps),
            _resident((1, d)),
        ],
        out_specs=_tile_spec(tm, d, nt, ntiles, steps),
        out_shape=jax.ShapeDtypeStruct(x.shape, x.dtype),
        scratch_shapes=[
            pltpu.VMEM((d, 2 * f), BF16),
            pltpu.VMEM((f, d), BF16),
            pltpu.VMEM((tm, d), BF16),
            pltpu.VMEM((tm, f), BF16),
            pltpu.VMEM((SUBLANES, f), F32),
        ],
        compiler_params=_compiler_params(),
        name="conv_ffn",
    )(x, g, w_in, dw_w, dw_b, w_out, final_g)


def _conv_module_jobs(win_ref, dww_ref, dwb_ref, hb_ref, buf_ref, y_ref, *, tm):
    d = D_MODEL
    base = CONV_HIST - (CONV_WIDTH - 1)
    rb = CONV_ROW_BLOCK
    cw = CONV_COL_TILE

    def glu(cb):
        hb = hb_ref[...]
        a = jnp.dot(hb, win_ref[:, cb:cb + cw], preferred_element_type=F32)
        gate = jnp.dot(hb, win_ref[:, d + cb:d + cb + cw], preferred_element_type=F32)
        buf_ref[CONV_HIST:CONV_HIST + tm, cb:cb + cw] = a * _sigmoid(gate)

    def conv_block(r0, c0):
        total = None
        for phase in range(SUBLANES):
            part = None
            for j in range(CONV_WIDTH):
                off = base + j
                if off % SUBLANES != phase:
                    continue
                term = (dww_ref[j:j + 1, c0:c0 + LANES]
                        * buf_ref[r0 + off:r0 + off + rb, c0:c0 + LANES])
                part = term if part is None else part + term
            if part is not None:
                total = part if total is None else total + part
        y_ref[r0:r0 + rb, c0:c0 + LANES] = total + dwb_ref[:, c0:c0 + LANES]

    jobs = []
    for cb in range(0, d, cw):
        jobs.append(functools.partial(glu, cb))
        for c0 in range(cb, cb + cw, LANES):
            for r0 in range(0, tm, rb):
                jobs.append(functools.partial(conv_block, r0, c0))
    return jobs


def _conv_ffn_fused_kernel(x_ref, gm_ref, cwin32_ref, cdww_ref, cdwb_ref, lng_ref, lnb_ref,
                           cwout32_ref, gf_ref, fwin32_ref, fdww_ref, fdwb_ref, fwout32_ref,
                           fg_ref, o_ref,
                           cwin_ref, cwout_ref, fwin_ref, fwout_ref,
                           x1_ref, hbc_ref, buf_ref, y_ref, hb_ref, act_ref, carry_ref,
                           *, tm, fc, nt, load_steps, final_norm):
    d = D_MODEL
    n = pl.program_id(0)
    _stage_weights(n, load_steps, [(cwin32_ref, cwin_ref), (cwout32_ref, cwout_ref),
                                   (fwin32_ref, fwin_ref), (fwout32_ref, fwout_ref)])
    k = n - load_steps

    @pl.when(k >= 0)
    def _():
        conv_seq_start = k % nt == 0
        ffn_seq_start = (k - 1) % nt == 0

        @pl.when(k == 0)
        def _():
            x1_ref[...] = jnp.zeros_like(x1_ref)

        @pl.when(conv_seq_start)
        def _():
            buf_ref[0:CONV_HIST, :] = jnp.zeros((CONV_HIST, d), F32)

        @pl.when(jnp.logical_not(conv_seq_start))
        def _():
            buf_ref[0:CONV_HIST, :] = buf_ref[tm:tm + CONV_HIST, :]

        @pl.when(jnp.logical_or(ffn_seq_start, k == 0))
        def _():
            carry_ref[...] = jnp.zeros_like(carry_ref)

        x1 = x1_ref[...]
        hb_ref[...] = _rmsnorm(x1, gf_ref[...]).astype(BF16)
        x = x_ref[0]
        hbc_ref[...] = _rmsnorm(x, gm_ref[...]).astype(BF16)

        jobs = _conv_module_jobs(cwin_ref, cdww_ref, cdwb_ref, hbc_ref, buf_ref, y_ref, tm=tm)
        per_chunk = -(-len(jobs) // (FFN_DIM // fc))

        def conv_jobs_after_chunk(c):
            for job in jobs[c * per_chunk:(c + 1) * per_chunk]:
                job()

        _ffn_hidden(hb_ref, fwin_ref, fdww_ref, fdwb_ref, act_ref, carry_ref, tm=tm, fc=fc,
                    after_chunk=conv_jobs_after_chunk)
        y = x1 + jnp.dot(act_ref[...], fwout_ref[...], preferred_element_type=F32)
        o_ref[0] = _rmsnorm(y, fg_ref[...]) if final_norm else y

        yn = _center_scale(y_ref[...]) * lng_ref[...] + lnb_ref[...]
        act = (yn * _sigmoid(yn)).astype(BF16)
        x1_ref[...] = x + jnp.dot(act, cwout_ref[...], preferred_element_type=F32)


def _conv_module_ffn(x, layer, mix_g, c_w_in, c_dw_w, c_dw_b, ln_g, ln_b, c_w_out,
                     ffn_g, f_w_in, f_dw_w, f_dw_b, f_w_out, final_g, *, final_norm,
                     tm=TOKEN_TILE, fc=FFN_COL_TILE):
    b, s, d = x.shape
    f = FFN_DIM
    nt = s // tm
    ntiles = b * nt
    steps = FUSED_WEIGHT_LOAD_STEPS
    return pl.pallas_call(
        functools.partial(_conv_ffn_fused_kernel, tm=tm, fc=fc, nt=nt, load_steps=steps,
                          final_norm=final_norm),
        grid=(steps + ntiles + 1,),
        in_specs=[
            _tile_spec(tm, d, nt, ntiles, steps),
            _layer_resident((1, d), layer),
            _weight_chunk_spec((d, 2 * d), layer // 2, steps),
            _layer_resident((CONV_WIDTH, d), layer // 2),
            _layer_resident((1, d), layer // 2),
            _layer_resident((1, d), layer // 2),
            _layer_resident((1, d), layer // 2),
            _weight_chunk_spec((d, d), layer // 2, steps),
            _layer_resident((1, d), layer),
            _weight_chunk_spec((d, 2 * f), layer, steps),
            _layer_resident((FFN_CONV_WIDTH, f), layer),
            _layer_resident((1, f), layer),
            _weight_chunk_spec((f, d), layer, steps),
            _resident((1, d)),
        ],
        out_specs=_tile_spec(tm, d, nt, ntiles, steps, lag=1),
        out_shape=jax.ShapeDtypeStruct(x.shape, x.dtype),
        scratch_shapes=[
            pltpu.VMEM((d, 2 * d), BF16),
            pltpu.VMEM((d, d), BF16),
            pltpu.VMEM((d, 2 * f), BF16),
            pltpu.VMEM((f, d), BF16),
            pltpu.VMEM((tm, d), F32),
            pltpu.VMEM((tm, d), BF16),
            pltpu.VMEM((CONV_HIST + tm, d), F32),
            pltpu.VMEM((tm, d), F32),
            pltpu.VMEM((tm, d), BF16),
            pltpu.VMEM((tm, f), BF16),
            pltpu.VMEM((SUBLANES, f), F32),
        ],
        compiler_params=_compiler_params(),
        name="conv_module_ffn",
    )(x, mix_g, c_w_in, c_dw_w, c_dw_b, ln_g, ln_b, c_w_out,
      ffn_g, f_w_in, f_dw_w, f_dw_b, f_w_out, final_g)


def _rope_kernel(pos_ref, invf_ref, cos_ref, sin_ref):
    ang = pos_ref[0] * invf_ref[...]
    cos_ref[0] = jnp.cos(ang)
    sin_ref[0] = jnp.sin(ang)


def _rope_tables(positions, *, tm=TOKEN_TILE):
    b, s = positions.shape
    half = RET_HEAD_QK // 2
    inv_freq = 1.0 / (ROPE_BASE ** (jnp.arange(0, RET_HEAD_QK, 2, dtype=F32) / RET_HEAD_QK))

    def tile(width):
        return pl.BlockSpec((1, tm, width), lambda bi, i: (bi, i, 0))

    table = jax.ShapeDtypeStruct((b, s, half), F32)
    return pl.pallas_call(
        _rope_kernel,
        grid=(b, s // tm),
        in_specs=[tile(1), pl.BlockSpec((1, half), lambda bi, i: (0, 0))],
        out_specs=[tile(half), tile(half)],
        out_shape=[table, table],
        compiler_params=pltpu.CompilerParams(
            dimension_semantics=("arbitrary", "arbitrary"),
            vmem_limit_bytes=VMEM_LIMIT_BYTES,
        ),
        name="rope_tables",
    )(positions.astype(F32).reshape(b, s, 1), inv_freq.reshape(1, half))


def _ret_proj_kernel(x_ref, g_ref, win32_ref, cos_ref, sin_ref,
                     q_ref, k_ref, v_ref, sg_ref, win_ref, hb_ref, *, load_steps):
    n = pl.program_id(0)
    _stage_weights(n, load_steps, [(win32_ref, win_ref)])

    @pl.when(n >= load_steps)
    def _():
        hb_ref[...] = _rmsnorm(x_ref[0], g_ref[...]).astype(BF16)
        half = RET_HEAD_QK // 2
        cos = cos_ref[0]
        sin = sin_ref[0]
        v0 = 2 * RET_QK_DIM
        g0 = v0 + RET_V_DIM
        nw = RET_PROJ_COL_TILE
        for lo in range(0, RET_V_DIM, nw):
            v = jnp.dot(hb_ref[...], win_ref[:, v0 + lo:v0 + lo + nw],
                        preferred_element_type=F32)
            v_ref[0, :, lo:lo + nw] = v.astype(BF16)
            gt = jnp.dot(hb_ref[...], win_ref[:, g0 + lo:g0 + lo + nw],
                         preferred_element_type=F32)
            sg_ref[0, :, lo:lo + nw] = (gt * _sigmoid(gt)).astype(BF16)

        def rotary(col, scale, out_ref, out_col):
            t = jnp.dot(hb_ref[...], win_ref[:, col:col + RET_HEAD_QK],
                        preferred_element_type=F32)
            t1, t2 = t[:, :half], t[:, half:]
            out_ref[0, :, out_col:out_col + half] = ((t1 * cos - t2 * sin) * scale).astype(BF16)
            out_ref[0, :, out_col + half:out_col + RET_HEAD_QK] = (
                (t2 * cos + t1 * sin) * scale).astype(BF16)

        for h in range(RET_HEADS):
            rotary(h * RET_HEAD_QK, 1.0, q_ref, h * RET_HEAD_QK)
            rotary(RET_QK_DIM + h * RET_HEAD_QK, RET_HEAD_QK ** -0.5, k_ref, h * RET_HEAD_QK)


def _ret_proj(x, layer, g, w_in, cos, sin, *, tm=TOKEN_TILE):
    b, s, d = x.shape
    half = RET_HEAD_QK // 2
    width = 2 * RET_QK_DIM + 2 * RET_V_DIM
    nt = s // tm
    ntiles = b * nt
    steps = WEIGHT_LOAD_STEPS
    tile = functools.partial(_tile_spec, tm, nt=nt, ntiles=ntiles, steps=steps)
    return pl.pallas_call(
        functools.partial(_ret_proj_kernel, load_steps=steps),
        grid=(steps + ntiles,),
        in_specs=[
            tile(d),
            _layer_resident((1, d), layer),
            _weight_chunk_spec((d, width), layer // 2, steps),
            tile(half),
            tile(half),
        ],
        out_specs=[tile(RET_QK_DIM), tile(RET_QK_DIM), tile(RET_V_DIM), tile(RET_V_DIM)],
        out_shape=[
            jax.ShapeDtypeStruct((b, s, RET_QK_DIM), BF16),
            jax.ShapeDtypeStruct((b, s, RET_QK_DIM), BF16),
            jax.ShapeDtypeStruct((b, s, RET_V_DIM), BF16),
            jax.ShapeDtypeStruct((b, s, RET_V_DIM), BF16),
        ],
        scratch_shapes=[
            pltpu.VMEM((d, width), BF16),
            pltpu.VMEM((tm, d), BF16),
        ],
        compiler_params=_compiler_params(),
        name="ret_proj",
    )(x, g, w_in, cos, sin)


def _ret_scan_kernel(x_ref, q_ref, k_ref, v_ref, sg_ref, intra_ref, qd_ref, kd_ref, cd_ref,
                     gng_ref, wout32_ref, o_ref, wout_ref, state_ref, og_ref,
                     *, tm, nt, load_steps):
    n = pl.program_id(0)
    _stage_weights(n, load_steps, [(wout32_ref, wout_ref)])
    tile = n - load_steps

    @pl.when(tile >= 0)
    def _():
        @pl.when(tile % nt == 0)
        def _():
            state_ref[...] = jnp.zeros_like(state_ref)

        c = RET_CHUNK
        for h in range(RET_HEADS):
            qk = slice(h * RET_HEAD_QK, (h + 1) * RET_HEAD_QK)
            vv = slice(h * RET_HEAD_V, (h + 1) * RET_HEAD_V)
            qd = qd_ref[h]
            kd = kd_ref[h]
            for i in range(tm // c):
                rows = slice(i * c, (i + 1) * c)
                qi = q_ref[0, rows, qk]
                ki = k_ref[0, rows, qk]
                vi = v_ref[0, rows, vv]
                scores = lax.dot_general(qi, ki, (((1,), (1,)), ((), ())),
                                         preferred_element_type=F32) * intra_ref[h]
                inner = jnp.dot(scores.astype(BF16), vi, preferred_element_type=F32)
                state = state_ref[h]
                cross = jnp.dot(qi, state.astype(BF16), preferred_element_type=F32) * qd
                kdec = (ki.astype(F32) * kd).astype(BF16)
                kv = lax.dot_general(kdec, vi, (((0,), (0,)), ((), ())),
                                     preferred_element_type=F32)
                state_ref[h] = cd_ref[h] * state + kv
                on = _center_scale(inner + cross) * gng_ref[:, vv]
                og_ref[rows, vv] = (sg_ref[0, rows, vv].astype(F32) * on).astype(BF16)

        o_ref[0] = x_ref[0] + jnp.dot(og_ref[...], wout_ref[...], preferred_element_type=F32)


def _decay_tables():
    c = RET_CHUNK
    log_gamma = jnp.log1p(-jnp.exp2(-5.0 - jnp.arange(RET_HEADS, dtype=F32)))
    idx = jnp.arange(c, dtype=F32)
    dist = idx[:, None] - idx[None, :]
    intra = jnp.where(dist[None] >= 0,
                      jnp.exp(log_gamma[:, None, None] * jnp.maximum(dist, 0.0)[None]), 0.0)
    q_decay = jnp.exp(log_gamma[:, None] * (idx + 1.0)[None])
    k_decay = jnp.exp(log_gamma[:, None] * (c - 1.0 - idx)[None])
    chunk_decay = jnp.exp(log_gamma * c)
    return (intra,
            jnp.broadcast_to(q_decay[:, :, None], (RET_HEADS, c, RET_HEAD_V)),
            jnp.broadcast_to(k_decay[:, :, None], (RET_HEADS, c, RET_HEAD_QK)),
            jnp.broadcast_to(chunk_decay[:, None, None], (RET_HEADS, 1, RET_HEAD_V)))


def _ret_scan(x, layer, q, k, v, sg, gn_g, w_out, *, tm=TOKEN_TILE):
    b, s, d = x.shape
    nt = s // tm
    ntiles = b * nt
    intra, qd, kd, cd = _decay_tables()
    steps = WEIGHT_LOAD_STEPS
    tile = functools.partial(_tile_spec, tm, nt=nt, ntiles=ntiles, steps=steps)
    return pl.pallas_call(
        functools.partial(_ret_scan_kernel, tm=tm, nt=nt, load_steps=steps),
        grid=(steps + ntiles,),
        in_specs=[
            tile(d),
            tile(RET_QK_DIM),
            tile(RET_QK_DIM),
            tile(RET_V_DIM),
            tile(RET_V_DIM),
            _resident(intra.shape),
            _resident(qd.shape),
            _resident(kd.shape),
            _resident(cd.shape),
            _layer_resident((1, RET_V_DIM), layer // 2),
            _weight_chunk_spec((RET_V_DIM, d), layer // 2, steps),
        ],
        out_specs=tile(d),
        out_shape=jax.ShapeDtypeStruct(x.shape, x.dtype),
        scratch_shapes=[
            pltpu.VMEM((RET_V_DIM, d), BF16),
            pltpu.VMEM((RET_HEADS, RET_HEAD_QK, RET_HEAD_V), F32),
            pltpu.VMEM((tm, RET_V_DIM), BF16),
        ],
        compiler_params=_compiler_params(),
        name="ret_scan",
    )(x, q, k, v, sg, intra, qd, kd, cd, gn_g, w_out)


def kernel(x, positions, conv_w_in, conv_dw_w, conv_dw_b, conv_ln_g, conv_ln_b, conv_w_out,
           ret_w_in, ret_gn_g, ret_w_out, ffn_w_in, ffn_dw_w, ffn_dw_b, ffn_w_out,
           norm_mix_g, norm_ffn_g, final_g):
    depth = norm_mix_g.shape[0]
    d = x.shape[-1]
    conv_dw_b, conv_ln_g, conv_ln_b = _rows(conv_dw_b), _rows(conv_ln_g), _rows(conv_ln_b)
    ret_gn_g, ffn_dw_b = _rows(ret_gn_g), _rows(ffn_dw_b)
    norm_mix_g, norm_ffn_g = _rows(norm_mix_g), _rows(norm_ffn_g)
    final_g = final_g.reshape(1, d)
    cos, sin = _rope_tables(positions)

    for i in range(depth):
        final_norm = i == depth - 1
        if i % 2 == 0:
            x = _conv_module_ffn(x, i, norm_mix_g, conv_w_in, conv_dw_w, conv_dw_b,
                                 conv_ln_g, conv_ln_b, conv_w_out,
                                 norm_ffn_g, ffn_w_in, ffn_dw_w, ffn_dw_b, ffn_w_out, final_g,
                                 final_norm=final_norm)
        else:
            q, k, v, sg = _ret_proj(x, i, norm_mix_g, ret_w_in, cos, sin)
            x = _ret_scan(x, i, q, k, v, sg, ret_gn_g, ret_w_out)
            x = _conv_ffn(x, i, norm_ffn_g, ffn_w_in, ffn_dw_w, ffn_dw_b, ffn_w_out, final_g,
                          final_norm=final_norm)
    return x
```
